```python
import math
import jax, jax.numpy as jnp
from jax import lax
import numpy as np

D_MODEL = 1024
BATCH = 8
SEQ = 4096
DEPTH = 2

GRID_W = 64
EPS = 1e-6
ROPE_THETA = 10000.0
Q_BLOCK = 128

MLA_HEADS = 6
MLA_NOPE = 64
MLA_ROPE = 32
MLA_V = 64
MLA_QK = MLA_NOPE + MLA_ROPE
Q_LORA = 256
KV_LORA = 128
MLA_WIDTH = MLA_HEADS * MLA_V

NA_HEADS = 6
NA_DIM = 64
NA_WIDTH = NA_HEADS * NA_DIM
NA_KR_MAX = 8
NA_KC = 16

DIFF_HEADS = 4
DIFF_QK = 32
DIFF_V = 2 * DIFF_QK
DIFF_WIDTH = DIFF_HEADS * DIFF_V

MIX_WIDTH = MLA_WIDTH + NA_WIDTH + DIFF_WIDTH

IN_SIZES = (Q_LORA, KV_LORA, MLA_ROPE, 3 * NA_WIDTH, 3 * DIFF_WIDTH, MLA_WIDTH, NA_WIDTH, DIFF_WIDTH)
N_IN = sum(IN_SIZES)
IN_SPLITS = tuple(sum(IN_SIZES[:i + 1]) for i in range(len(IN_SIZES) - 1))

kernel_name = "hybrid_mla_natten_diff_encoder"


def rms_norm(x, g):
    xf = x.astype(jnp.float32)
    y = xf * lax.rsqrt(jnp.mean(xf * xf, axis=-1, keepdims=True) + EPS)
    return (y * g.astype(jnp.float32)).astype(x.dtype)


def rope_tables(seq_len, dim):
    inv = ROPE_THETA ** (-jnp.arange(0, dim, 2, dtype=jnp.float32) / dim)
    ang = jnp.arange(seq_len, dtype=jnp.float32)[:, None] * inv[None, :]
    return jnp.cos(ang), jnp.sin(ang)


def apply_rope(x, cos, sin):
    half = x.shape[-1] // 2
    shape = (1, x.shape[1]) + (1,) * (x.ndim - 3) + (half,)
    cs, sn = cos.reshape(shape), sin.reshape(shape)
    xf = x.astype(jnp.float32)
    x1, x2 = xf[..., :half], xf[..., half:]
    return jnp.concatenate([x1 * cs - x2 * sn, x2 * cs + x1 * sn], axis=-1).astype(x.dtype)


def blocked_attention(q, k, v, scale):
    B, S, H, d = q.shape
    dv = v.shape[-1]
    nb = S // Q_BLOCK
    qb = q.reshape(B, nb, Q_BLOCK, H, d).transpose(1, 0, 2, 3, 4)

    def one(qi):
        s = jnp.einsum('bqhd,bkhd->bhqk', qi, k, preferred_element_type=jnp.float32) * scale
        p = jax.nn.softmax(s, axis=-1)
        return jnp.einsum('bhqk,bkhe->bqhe', p.astype(v.dtype), v)

    out = lax.map(one, qb)
    return out.transpose(1, 0, 2, 3, 4).reshape(B, S, H, dv)


def blocked_diff_attention(q, k, v, lam, scale):
    B, S, H, _, d = q.shape
    dv = v.shape[-1]
    nb = S // Q_BLOCK
    qb = q.reshape(B, nb, Q_BLOCK, H, 2, d).transpose(1, 0, 2, 3, 4, 5)

    def one(qi):
        s = jnp.einsum('bqhnd,bkhnd->bhnqk', qi, k, preferred_element_type=jnp.float32) * scale
        p = jax.nn.softmax(s, axis=-1)
        a = p[:, :, 0] - lam * p[:, :, 1]
        return jnp.einsum('bhqk,bkhe->bqhe', a.astype(v.dtype), v)

    out = lax.map(one, qb)
    return out.transpose(1, 0, 2, 3, 4).reshape(B, S, H, dv)


def neighborhood_attention(q, k, v, rpb, scale):
    B, S, H, d = q.shape
    rows = S // GRID_W
    kr = min(NA_KR_MAX, rows)
    kc = NA_KC
    qg = q.reshape(B, rows, GRID_W, H, d)
    kg = k.reshape(B, rows, GRID_W, H, d)
    vg = v.reshape(B, rows, GRID_W, H, d)
    col = jnp.arange(GRID_W)
    col_start = jnp.clip(col - kc // 2, 0, GRID_W - kc)
    col_idx = col_start[:, None] + jnp.arange(kc)[None, :]
    dc = col_idx - col[:, None] + (NA_KC - 1)

    def one(r):
        r0 = jnp.clip(r - kr // 2, 0, rows - kr)
        q_r = lax.dynamic_index_in_dim(qg, r, axis=1, keepdims=False)
        k_r = lax.dynamic_slice_in_dim(kg, r0, kr, axis=1)[:, :, col_idx]
        v_r = lax.dynamic_slice_in_dim(vg, r0, kr, axis=1)[:, :, col_idx]
        dr = r0 + jnp.arange(kr) - r + (NA_KR_MAX - 1)
        bias = rpb[:, dr[:, None, None], dc[None, :, :]]
        s = jnp.einsum('bchd,brcjhd->bhcrj', q_r, k_r, preferred_element_type=jnp.float32) * scale
        s = s + bias.transpose(0, 2, 1, 3)[None].astype(jnp.float32)
        p = jax.nn.softmax(s.reshape(B, H, GRID_W, kr * kc), axis=-1).reshape(B, H, GRID_W, kr, kc)
        return jnp.einsum('bhcrj,brcjhd->bchd', p.astype(v.dtype), v_r)

    out = lax.map(one, jnp.arange(rows))
    return out.transpose(1, 0, 2, 3, 4).reshape(B, S, H, d)


def hybrid_layer(x, c, layer_idx, ada_w, ada_b, norm_g, w_in, q_lat_g, w_uq, kv_lat_g, w_ukv,
                 mla_q_g, mla_k_g, na_q_g, na_k_g, na_rpb, diff_q_g, diff_k_g,
                 lam_q1, lam_k1, lam_q2, lam_k2, subln_g, w_out,
                 cos_mla, sin_mla, cos_diff, sin_diff):
    B, S, _ = x.shape
    mod = jax.nn.silu(c) @ ada_w + ada_b
    shift, scale, gate = jnp.split(mod, 3, axis=-1)
    h = rms_norm(x, norm_g) * (1.0 + scale[:, None, :]) + shift[:, None, :]

    proj = h @ w_in
    c_q, c_kv, k_pe, na_qkv, diff_qkv, g_mla, g_na, g_diff = jnp.split(proj, IN_SPLITS, axis=-1)

    q = (rms_norm(c_q, q_lat_g) @ w_uq).reshape(B, S, MLA_HEADS, MLA_QK)
    kv = (rms_norm(c_kv, kv_lat_g) @ w_ukv).reshape(B, S, MLA_HEADS, MLA_NOPE + MLA_V)
    k_nope, v_mla = kv[..., :MLA_NOPE], kv[..., MLA_NOPE:]
    k = jnp.concatenate([k_nope, jnp.broadcast_to(k_pe[:, :, None, :], (B, S, MLA_HEADS, MLA_ROPE))], axis=-1)
    q = rms_norm(q, mla_q_g)
    k = rms_norm(k, mla_k_g)
    q = jnp.concatenate([q[..., :MLA_NOPE], apply_rope(q[..., MLA_NOPE:], cos_mla, sin_mla)], axis=-1)
    k = jnp.concatenate([k[..., :MLA_NOPE], apply_rope(k[..., MLA_NOPE:], cos_mla, sin_mla)], axis=-1)
    o_mla = blocked_attention(q, k, v_mla, MLA_QK ** -0.5).reshape(B, S, MLA_WIDTH)

    qn, kn, vn = jnp.split(na_qkv, 3, axis=-1)
    qn = rms_norm(qn.reshape(B, S, NA_HEADS, NA_DIM), na_q_g)
    kn = rms_norm(kn.reshape(B, S, NA_HEADS, NA_DIM), na_k_g)
    vn = vn.reshape(B, S, NA_HEADS, NA_DIM)
    o_na = neighborhood_attention(qn, kn, vn, na_rpb, NA_DIM ** -0.5).reshape(B, S, NA_WIDTH)

    qd, kd, vd = jnp.split(diff_qkv, 3, axis=-1)
    qd = apply_rope(rms_norm(qd.reshape(B, S, DIFF_HEADS, 2, DIFF_QK), diff_q_g), cos_diff, sin_diff)
    kd = apply_rope(rms_norm(kd.reshape(B, S, DIFF_HEADS, 2, DIFF_QK), diff_k_g), cos_diff, sin_diff)
    vd = vd.reshape(B, S, DIFF_HEADS, DIFF_V)
    lam_init = 0.8 - 0.6 * math.exp(-0.3 * layer_idx)
    f32 = jnp.float32
    lam = (jnp.exp(jnp.sum(lam_q1.astype(f32) * lam_k1.astype(f32)))
           - jnp.exp(jnp.sum(lam_q2.astype(f32) * lam_k2.astype(f32))) + lam_init)
    o_d = blocked_diff_attention(qd, kd, vd, lam, DIFF_QK ** -0.5)
    o_d = (rms_norm(o_d, subln_g) * (1.0 - lam_init)).reshape(B, S, DIFF_WIDTH)

    y = jnp.concatenate([o_mla * jax.nn.silu(g_mla),
                         o_na * jax.nn.silu(g_na),
                         o_d * jax.nn.silu(g_diff)], axis=-1) @ w_out
    return x + gate[:, None, :] * y


def setup_inputs(seed: int = 0) -> dict:
    key = jax.random.key(seed)
    ks = jax.random.split(key, 24)
    f32 = jnp.float32
    L = DEPTH

    def nrm(k, shape, s):
        return jax.random.normal(k, shape, f32) * s

    def gain(k, shape):
        return 1.0 + 0.05 * jax.random.normal(k, shape, f32)

    return {
        "x": jax.random.normal(ks[0], (BATCH, SEQ, D_MODEL), f32),
        "c": jax.random.normal(ks[1], (BATCH, D_MODEL), f32),
        "ada_w": nrm(ks[2], (L, D_MODEL, 3 * D_MODEL), 0.5 * D_MODEL ** -0.5),
        "ada_b": nrm(ks[3], (L, 3 * D_MODEL), 0.02),
        "norm_g": gain(ks[4], (L, D_MODEL)),
        "w_in": nrm(ks[5], (L, D_MODEL, N_IN), D_MODEL ** -0.5),
        "q_lat_g": gain(ks[6], (L, Q_LORA)),
        "w_uq": nrm(ks[7], (L, Q_LORA, MLA_HEADS * MLA_QK), Q_LORA ** -0.5),
        "kv_lat_g": gain(ks[8], (L, KV_LORA)),
        "w_ukv": nrm(ks[9], (L, KV_LORA, MLA_HEADS * (MLA_NOPE + MLA_V)), KV_LORA ** -0.5),
        "mla_q_g": gain(ks[10], (L, MLA_QK)),
        "mla_k_g": gain(ks[11], (L, MLA_QK)),
        "na_q_g": gain(ks[12], (L, NA_DIM)),
        "na_k_g": gain(ks[13], (L, NA_DIM)),
        "na_rpb": nrm(ks[14], (L, NA_HEADS, 2 * NA_KR_MAX - 1, 2 * NA_KC - 1), 0.1),
        "diff_q_g": gain(ks[15], (L, DIFF_QK)),
        "diff_k_g": gain(ks[16], (L, DIFF_QK)),
        "lam_q1": nrm(ks[17], (L, DIFF_QK), 0.1),
        "lam_k1": nrm(ks[18], (L, DIFF_QK), 0.1),
        "lam_q2": nrm(ks[19], (L, DIFF_QK), 0.1),
        "lam_k2": nrm(ks[20], (L, DIFF_QK), 0.1),
        "subln_g": gain(ks[21], (L, DIFF_V)),
        "w_out": nrm(ks[22], (L, MIX_WIDTH, D_MODEL), MIX_WIDTH ** -0.5),
    }


def reference(x, c, ada_w, ada_b, norm_g, w_in, q_lat_g, w_uq, kv_lat_g, w_ukv,
              mla_q_g, mla_k_g, na_q_g, na_k_g, na_rpb, diff_q_g, diff_k_g,
              lam_q1, lam_k1, lam_q2, lam_k2, subln_g, w_out):
    S = x.shape[1]
    cos_mla, sin_mla = rope_tables(S, MLA_ROPE)
    cos_diff, sin_diff = rope_tables(S, DIFF_QK)
    h = x
    for i in range(DEPTH):
        h = hybrid_layer(h, c, i, ada_w[i], ada_b[i], norm_g[i], w_in[i], q_lat_g[i], w_uq[i],
                         kv_lat_g[i], w_ukv[i], mla_q_g[i], mla_k_g[i], na_q_g[i], na_k_g[i],
                         na_rpb[i], diff_q_g[i], diff_k_g[i], lam_q1[i], lam_k1[i], lam_q2[i],
                         lam_k2[i], subln_g[i], w_out[i], cos_mla, sin_mla, cos_diff, sin_diff)
    return h
```

```python
import functools
import math

import numpy as np
import jax
import jax.numpy as jnp
from jax import lax
from jax.experimental import pallas as pl
from jax.experimental.pallas import tpu as pltpu

F32 = jnp.float32
BF16 = jnp.bfloat16

D_MODEL = 1024
GRID_W = 64
EPS = 1e-6
ROPE_THETA = 10000.0
MLA_HEADS, MLA_NOPE, MLA_ROPE, MLA_V = 6, 64, 32, 64
MLA_QK = MLA_NOPE + MLA_ROPE
Q_LORA, KV_LORA = 256, 128
NA_HEADS, NA_DIM, NA_KR, NA_KC = 6, 64, 8, 16
DIFF_HEADS, DIFF_QK = 4, 32
DIFF_V = 2 * DIFF_QK
MLA_WIDTH, NA_WIDTH, DIFF_WIDTH = MLA_HEADS * MLA_V, NA_HEADS * NA_DIM, DIFF_HEADS * DIFF_V

LANES = 128
VMEM_LIMIT = 48 * 1024 * 1024

C_CQ, C_CKV, C_KPE, C_KPER = 0, 256, 384, 512
C_NA = 640
C_DIFF = 1792
C_GATE = 3072
N_EXT = 4096

G_NORM, G_QLAT, G_KVLAT, G_MQ, G_MQR, G_MK, G_MKR, G_NQ, G_NK, G_DQ, G_DQR, G_DK, G_DKR = range(13)
G_ROWS = 16

NA_RB = 4
NA_KROWS = 12
NEG = -1e30


def _nt(a, b):
    return lax.dot_general(a, b, (((1,), (1,)), ((), ())), preferred_element_type=F32)


def _mm(a, b):
    return jnp.dot(a, b, preferred_element_type=F32)


def _seg_ones(seg):
    sh = int(math.log2(seg))
    r = lax.broadcasted_iota(jnp.int32, (LANES, LANES), 0) >> sh
    c = lax.broadcasted_iota(jnp.int32, (LANES, LANES), 1) >> sh
    return jnp.where(r == c, 1.0, 0.0).astype(BF16)


def _seg_sumsq(x, ones):
    x2 = x * x
    hi = x2.astype(BF16)
    lo = (x2 - hi.astype(F32)).astype(BF16)
    return _mm(hi, ones) + _mm(lo, ones)


def _adaln_body(c_ref, w_ref, b_ref, o_ref):
    c = c_ref[...]
    a = c / (1.0 + jnp.exp(-c))
    o_ref[0, 0] = jnp.dot(a, w_ref[0], preferred_element_type=F32,
                          precision=lax.Precision.HIGHEST) + b_ref[0, 0]


def _adaln(c, ada_w, ada_b):
    L = ada_w.shape[0]
    B = c.shape[0]
    b4 = ada_b.reshape(L, 3, 1, D_MODEL)
    return pl.pallas_call(
        _adaln_body,
        out_shape=jax.ShapeDtypeStruct((L, 3, B, D_MODEL), F32),
        grid=(L, 3),
        in_specs=[
            pl.BlockSpec((B, D_MODEL), lambda l, j: (0, 0)),
            pl.BlockSpec((1, D_MODEL, D_MODEL), lambda l, j: (l, 0, j)),
            pl.BlockSpec((1, 1, 1, D_MODEL), lambda l, j: (l, j, 0, 0)),
        ],
        out_specs=pl.BlockSpec((1, 1, B, D_MODEL), lambda l, j: (l, j, 0, 0)),
        compiler_params=pltpu.CompilerParams(
            dimension_semantics=("arbitrary", "arbitrary"), vmem_limit_bytes=VMEM_LIMIT),
        name="adaln",
    )(c, ada_w, b4)


def _inproj_body(x_ref, mod_ref, gv_ref, w_ref, wuq_ref, wukv_ref, cm_ref, sm_ref, cd_ref, sd_ref,
                 qm_ref, km_ref, vm_ref, nq_ref, nk_ref, nv_ref, dq_ref, dk_ref, dv_ref, gt_ref):
    def gain(row, width, off=0):
        return gv_ref[row:row + 1, off:off + width]

    x = x_ref[0]
    ms = jnp.mean(x * x, axis=-1, keepdims=True)
    xn = x * lax.rsqrt(ms + EPS) * gain(G_NORM, D_MODEL)
    h = (xn * (1.0 + mod_ref[0, 1:2, :]) + mod_ref[0, 0:1, :]).astype(BF16)

    def proj(a, b):
        return _mm(h, w_ref[:, a:b])

    def lat_norm(v, row, width):
        r = lax.rsqrt(jnp.mean(v * v, axis=-1, keepdims=True) + EPS)
        return (v * r * gain(row, width)).astype(BF16)

    cm, sm = cm_ref[...], sm_ref[...]
    cd, sd = cd_ref[...], sd_ref[...]

    cqn = lat_norm(proj(C_CQ, C_CQ + Q_LORA), G_QLAT, Q_LORA)
    qq = _mm(cqn, wuq_ref[...])
    wq_c = gain(G_MQ, LANES) * cm
    wq_s = gain(G_MQR, LANES) * sm
    q_scale = MLA_QK ** -0.5
    for hd in range(MLA_HEADS):
        a = qq[:, hd * LANES:(hd + 1) * LANES]
        b = qq[:, (MLA_HEADS + hd) * LANES:(MLA_HEADS + hd + 1) * LANES]
        r = lax.rsqrt(jnp.sum(a * a, axis=-1, keepdims=True) * (1.0 / MLA_QK) + EPS)
        qm_ref[0, :, hd * LANES:(hd + 1) * LANES] = ((r * q_scale) * (a * wq_c + b * wq_s)).astype(BF16)

    ckvn = lat_norm(proj(C_CKV, C_CKV + KV_LORA), G_KVLAT, KV_LORA)
    kv = _mm(ckvn, wukv_ref[...])
    vm_ref[0] = kv[:, MLA_HEADS * LANES:].astype(BF16)
    kpe2 = proj(C_KPE, C_KPE + 2 * LANES)
    kpe, kper = kpe2[:, :LANES], kpe2[:, LANES:]
    wk_c = gain(G_MK, LANES) * cm
    k_rot = kper * (gain(G_MKR, LANES) * sm)
    for hd in range(MLA_HEADS):
        a = kv[:, hd * LANES:(hd + 1) * LANES] + kpe
        r = lax.rsqrt(jnp.sum(a * a, axis=-1, keepdims=True) * (1.0 / MLA_QK) + EPS)
        km_ref[0, :, hd * LANES:(hd + 1) * LANES] = (r * (a * wk_c + k_rot)).astype(BF16)

    nqkv = proj(C_NA, C_NA + 3 * NA_WIDTH)
    ones64 = _seg_ones(NA_DIM)
    na_scale = NA_DIM ** -0.5
    for blk in range(NA_WIDTH // LANES):
        sl = slice(blk * LANES, (blk + 1) * LANES)
        a = nqkv[:, sl]
        r = lax.rsqrt(_seg_sumsq(a, ones64) * (1.0 / NA_DIM) + EPS)
        nq_ref[0, :, sl] = (a * (r * na_scale) * gain(G_NQ, LANES, blk * LANES)).astype(BF16)
        a = nqkv[:, NA_WIDTH + blk * LANES:NA_WIDTH + (blk + 1) * LANES]
        r = lax.rsqrt(_seg_sumsq(a, ones64) * (1.0 / NA_DIM) + EPS)
        nk_ref[0, :, sl] = (a * r * gain(G_NK, LANES, blk * LANES)).astype(BF16)
    nv_ref[0] = nqkv[:, 2 * NA_WIDTH:].astype(BF16)

    dd = proj(C_DIFF, C_DIFF + 5 * DIFF_WIDTH)
    ones32 = _seg_ones(DIFF_QK)
    d_scale = DIFF_QK ** -0.5
    for blk in range(DIFF_WIDTH // LANES):
        sl = slice(blk * LANES, (blk + 1) * LANES)
        for (o_ref, base, rot, grow, grot, sc) in (
                (dq_ref, 0, 3 * DIFF_WIDTH, G_DQ, G_DQR, d_scale),
                (dk_ref, DIFF_WIDTH, 4 * DIFF_WIDTH, G_DK, G_DKR, 1.0)):
            a = dd[:, base + blk * LANES:base + (blk + 1) * LANES]
            b = dd[:, rot + blk * LANES:rot + (blk + 1) * LANES]
            r = lax.rsqrt(_seg_sumsq(a, ones32) * (1.0 / DIFF_QK) + EPS)
            o_ref[0, :, sl] = ((r * sc) * (a * (gain(grow, LANES, blk * LANES) * cd)
                                           + b * (gain(grot, LANES, blk * LANES) * sd))).astype(BF16)
    dv_ref[0] = dd[:, 2 * DIFF_WIDTH:3 * DIFF_WIDTH].astype(BF16)

    g = proj(C_GATE, N_EXT)
    gt_ref[0] = (g / (1.0 + jnp.exp(-g))).astype(BF16)


def _inproj(x, mod_b, gv, w_ext, wuq, wukv, tabs, tm):
    B, S, _ = x.shape
    cm, sm, cd, sd = tabs
    widths = (MLA_HEADS * LANES, MLA_HEADS * LANES, MLA_WIDTH, NA_WIDTH, NA_WIDTH, NA_WIDTH,
              DIFF_WIDTH, DIFF_WIDTH, DIFF_WIDTH, D_MODEL)
    tok = lambda w: pl.BlockSpec((1, tm, w), lambda b, i: (b, i, 0))
    full = lambda a: pl.BlockSpec(a.shape, lambda b, i: (0,) * a.ndim)
    tab = pl.BlockSpec((tm, LANES), lambda b, i: (i, 0))
    return pl.pallas_call(
        _inproj_body,
        out_shape=[jax.ShapeDtypeStruct((B, S, w), BF16) for w in widths],
        grid=(B, S // tm),
        in_specs=[tok(D_MODEL), pl.BlockSpec((1, 3, D_MODEL), lambda b, i: (b, 0, 0)),
                  full(gv), full(w_ext), full(wuq), full(wukv), tab, tab, tab, tab],
        out_specs=[tok(w) for w in widths],
        compiler_params=pltpu.CompilerParams(
            dimension_semantics=("parallel", "parallel"), vmem_limit_bytes=VMEM_LIMIT),
        name="inproj",
    )(x, mod_b, gv, w_ext, wuq, wukv, cm, sm, cd, sd)


def _online(s, m, l):
    mn = jnp.maximum(m, jnp.max(s, axis=-1, keepdims=True))
    a = jnp.exp(m - mn)
    p = jnp.exp(s - mn)
    return mn, a * l + jnp.sum(p, axis=-1, keepdims=True), a, p.astype(BF16)


def _mla_body(q_ref, k_ref, v_ref, o_ref, *, tk, nk):
    tq = q_ref.shape[1]
    q0, q1 = q_ref[0, :, :LANES], q_ref[0, :, LANES:]
    lo = lax.broadcasted_iota(jnp.int32, (1, LANES), 1) < MLA_V

    def step(j, carry):
        m0, l0, m1, l1, acc = carry
        off = pl.multiple_of(j * tk, tk)
        k = k_ref[0, pl.ds(off, tk), :]
        v = v_ref[0, pl.ds(off, tk), :]
        zero = jnp.zeros_like(v)
        m0, l0, a0, p0 = _online(_nt(q0, k[:, :LANES]), m0, l0)
        m1, l1, a1, p1 = _online(_nt(q1, k[:, LANES:]), m1, l1)
        acc = (acc * jnp.where(lo, a0, a1)
               + _mm(p0, jnp.where(lo, v, zero)) + _mm(p1, jnp.where(lo, zero, v)))
        return m0, l0, m1, l1, acc

    mi = jnp.full((tq, 1), NEG, F32)
    li = jnp.zeros((tq, 1), F32)
    m0, l0, m1, l1, acc = lax.fori_loop(0, nk, step, (mi, li, mi, li, jnp.zeros((tq, LANES), F32)))
    o_ref[0] = (acc * jnp.where(lo, 1.0 / l0, 1.0 / l1)).astype(o_ref.dtype)


def _mla_attn(q, k, v, tq, tk):
    B, S, _ = q.shape
    return pl.pallas_call(
        functools.partial(_mla_body, tk=tk, nk=S // tk),
        out_shape=jax.ShapeDtypeStruct((B, S, MLA_WIDTH), BF16),
        grid=(B, MLA_HEADS // 2, S // tq),
        in_specs=[pl.BlockSpec((1, tq, 2 * LANES), lambda b, p, i: (b, i, p)),
                  pl.BlockSpec((1, S, 2 * LANES), lambda b, p, i: (b, 0, p)),
                  pl.BlockSpec((1, S, LANES), lambda b, p, i: (b, 0, p))],
        out_specs=pl.BlockSpec((1, tq, LANES), lambda b, p, i: (b, i, p)),
        compiler_params=pltpu.CompilerParams(
            dimension_semantics=("parallel", "parallel", "arbitrary"), vmem_limit_bytes=VMEM_LIMIT),
        name="mla_attn",
    )(q, k, v)


def _diff_body(q_ref, k_ref, v_ref, lam_ref, sg_ref, o_ref, *, tk, nk, lam_init):
    tq = q_ref.shape[1]
    q = q_ref[0]
    lane = lax.broadcasted_iota(jnp.int32, (1, LANES), 1)
    lo = lane < DIFF_V
    zq = jnp.zeros_like(q)
    qs = [jnp.where((lane >= g * DIFF_QK) & (lane < (g + 1) * DIFF_QK), q, zq) for g in range(4)]

    def step(j, carry):
        ms, ls, acc1, acc2 = carry
        off = pl.multiple_of(j * tk, tk)
        k = k_ref[0, pl.ds(off, tk), :]
        v = v_ref[0, pl.ds(off, tk), :]
        zero = jnp.zeros_like(v)
        vh = (jnp.where(lo, v, zero), jnp.where(lo, zero, v))
        new_m, new_l, al, pv = [], [], [], []
        for g in range(4):
            mn, ln, a, p = _online(_nt(qs[g], k), ms[g], ls[g])
            new_m.append(mn); new_l.append(ln); al.append(a)
            pv.append(_mm(p, vh[g // 2]))
        acc1 = acc1 * jnp.where(lo, al[0], al[2]) + pv[0] + pv[2]
        acc2 = acc2 * jnp.where(lo, al[1], al[3]) + pv[1] + pv[3]
        return tuple(new_m), tuple(new_l), acc1, acc2

    mi = jnp.full((tq, 1), NEG, F32)
    li = jnp.zeros((tq, 1), F32)
    za = jnp.zeros((tq, LANES), F32)
    ms, ls, acc1, acc2 = lax.fori_loop(0, nk, step, ((mi,) * 4, (li,) * 4, za, za))

    lv = lam_ref[...]
    lam = (jnp.exp(jnp.sum(lv[0:1] * lv[1:2], axis=-1, keepdims=True))
           - jnp.exp(jnp.sum(lv[2:3] * lv[3:4], axis=-1, keepdims=True)) + lam_init)
    o = (acc1 * jnp.where(lo, 1.0 / ls[0], 1.0 / ls[2])
         - lam * (acc2 * jnp.where(lo, 1.0 / ls[1], 1.0 / ls[3])))
    o2 = o * o
    ss0 = jnp.sum(jnp.where(lo, o2, 0.0), axis=-1, keepdims=True)
    ss1 = jnp.sum(jnp.where(lo, 0.0, o2), axis=-1, keepdims=True)
    r = lax.rsqrt(jnp.where(lo, ss0, ss1) * (1.0 / DIFF_V) + EPS)
    o_ref[0] = (o * r * (sg_ref[...] * (1.0 - lam_init))).astype(o_ref.dtype)


def _diff_attn(q, k, v, lamv, sg, lam_init, tq, tk):
    B, S, _ = q.shape
    return pl.pallas_call(
        functools.partial(_diff_body, tk=tk, nk=S // tk, lam_init=lam_init),
        out_shape=jax.ShapeDtypeStruct((B, S, DIFF_WIDTH), BF16),
        grid=(B, DIFF_HEADS // 2, S // tq),
        in_specs=[pl.BlockSpec((1, tq, LANES), lambda b, p, i: (b, i, p)),
                  pl.BlockSpec((1, S, LANES), lambda b, p, i: (b, 0, p)),
                  pl.BlockSpec((1, S, LANES), lambda b, p, i: (b, 0, p)),
                  pl.BlockSpec(lamv.shape, lambda b, p, i: (0, 0)),
                  pl.BlockSpec(sg.shape, lambda b, p, i: (0, 0))],
        out_specs=pl.BlockSpec((1, tq, LANES), lambda b, p, i: (b, i, p)),
        compiler_params=pltpu.CompilerParams(
            dimension_semantics=("parallel", "parallel", "arbitrary"), vmem_limit_bytes=VMEM_LIMIT),
        name="diff_attn",
    )(q, k, v, lamv, sg)


def _na_body(q_ref, k_ref, v_ref, bm_ref, o_ref, *, nrb):
    rb = pl.program_id(2)
    variant = jnp.where(rb == 0, 0, jnp.where(rb == nrb - 1, 2, 1))
    start = NA_RB * jnp.clip(rb - 1, 0, nrb - 3)
    off = pl.multiple_of(start * GRID_W, GRID_W)
    nkeys = NA_KROWS * GRID_W
    q = q_ref[0]
    k = k_ref[0, pl.ds(off, nkeys), :]
    v = v_ref[0, pl.ds(off, nkeys), :]
    lo = lax.broadcasted_iota(jnp.int32, (1, LANES), 1) < NA_DIM
    zq, zv = jnp.zeros_like(q), jnp.zeros_like(v)
    out = None
    for hh in range(2):
        sel = lo if hh == 0 else jnp.logical_not(lo)
        s = _nt(jnp.where(sel, q, zq), k) + bm_ref[hh, variant]
        m = jnp.max(s, axis=-1, keepdims=True)
        p = jnp.exp(s - m)
        l = jnp.sum(p, axis=-1, keepdims=True)
        pv = _mm(p.astype(BF16), jnp.where(sel, v, zv)) * (1.0 / l)
        out = pv if out is None else out + pv
    o_ref[0] = out.astype(o_ref.dtype)


def _na_attn(q, k, v, bm):
    B, S, _ = q.shape
    nrb = S // (NA_RB * GRID_W)
    tq = NA_RB * GRID_W
    return pl.pallas_call(
        functools.partial(_na_body, nrb=nrb),
        out_shape=jax.ShapeDtypeStruct((B, S, NA_WIDTH), BF16),
        grid=(B, NA_HEADS // 2, nrb),
        in_specs=[pl.BlockSpec((1, tq, LANES), lambda b, p, i: (b, i, p)),
                  pl.BlockSpec((1, S, LANES), lambda b, p, i: (b, 0, p)),
                  pl.BlockSpec((1, S, LANES), lambda b, p, i: (b, 0, p)),
                  pl.BlockSpec((2, 3, tq, NA_KROWS * GRID_W), lambda b, p, i: (p, 0, 0, 0))],
        out_specs=pl.BlockSpec((1, tq, LANES), lambda b, p, i: (b, i, p)),
        compiler_params=pltpu.CompilerParams(
            dimension_semantics=("parallel", "parallel", "arbitrary"), vmem_limit_bytes=VMEM_LIMIT),
        name="na_attn",
    )(q, k, v, bm)


def _na_bias_table(rpb, rows):
    nrb = rows // NA_RB
    W = GRID_W
    ri = np.arange(NA_RB)[:, None, None, None]
    c = np.arange(W)[None, :, None, None]
    kri = np.arange(NA_KROWS)[None, None, :, None]
    kc = np.arange(W)[None, None, None, :]
    c0 = np.clip(c - NA_KC // 2, 0, W - NA_KC)
    col_ok = (kc >= c0) & (kc < c0 + NA_KC)
    dc = np.clip(kc - c + (NA_KC - 1), 0, 2 * NA_KC - 2)
    drs, oks = [], []
    for rb in (0, 1, nrb - 1):
        start = NA_RB * min(max(rb - 1, 0), nrb - 3)
        r = NA_RB * rb + ri
        r0 = np.clip(r - NA_KR // 2, 0, rows - NA_KR)
        kr = start + kri
        row_ok = (kr >= r0) & (kr < r0 + NA_KR)
        shape = (NA_RB, W, NA_KROWS, W)
        drs.append(np.broadcast_to(np.clip(kr - r + (NA_KR - 1), 0, 2 * NA_KR - 2), shape))
        oks.append(np.broadcast_to(row_ok & col_ok, shape))
    dr = np.stack(drs).reshape(3, NA_RB * W, NA_KROWS * W)
    ok = np.stack(oks).reshape(3, NA_RB * W, NA_KROWS * W)
    dcb = np.broadcast_to(dc, (3, NA_RB, W, NA_KROWS, W)).reshape(3, NA_RB * W, NA_KROWS * W)
    vals = rpb.astype(F32)[:, dr, dcb]
    return jnp.where(ok[None], vals, NEG)


def _outproj_body(x_ref, om_ref, on_ref, od_ref, gt_ref, gate_ref, w_ref, o_ref):
    def branch(o_ref_, a, b):
        mix = (o_ref_[0].astype(F32) * gt_ref[0, :, a:b].astype(F32)).astype(BF16)
        return _mm(mix, w_ref[a:b, :])

    y = (branch(om_ref, 0, MLA_WIDTH) + branch(on_ref, MLA_WIDTH, MLA_WIDTH + NA_WIDTH)
         + branch(od_ref, MLA_WIDTH + NA_WIDTH, D_MODEL))
    o_ref[0] = x_ref[0] + gate_ref[0, 2:3, :] * y


def _outproj(x, om, on, od, gt, mod_b, w_out, tm):
    B, S, _ = x.shape
    tok = lambda w: pl.BlockSpec((1, tm, w), lambda b, i: (b, i, 0))
    return pl.pallas_call(
        _outproj_body,
        out_shape=jax.ShapeDtypeStruct((B, S, D_MODEL), F32),
        grid=(B, S // tm),
        in_specs=[tok(D_MODEL), tok(MLA_WIDTH), tok(NA_WIDTH), tok(DIFF_WIDTH), tok(D_MODEL),
                  pl.BlockSpec((1, 3, D_MODEL), lambda b, i: (b, 0, 0)),
                  pl.BlockSpec(w_out.shape, lambda b, i: (0, 0))],
        out_specs=tok(D_MODEL),
        compiler_params=pltpu.CompilerParams(
            dimension_semantics=("parallel", "parallel"), vmem_limit_bytes=VMEM_LIMIT),
        name="outproj",
    )(x, om, on, od, gt, mod_b, w_out)


def _rot_cols(w, half):
    return jnp.concatenate([-w[..., half:], w[..., :half]], axis=-1)


def _swap_halves(g, half):
    return jnp.concatenate([g[..., half:], g[..., :half]], axis=-1)


def _pad_cols(w, left, total):
    return jnp.pad(w, ((0, 0), (left, total - left - w.shape[-1])))


def _prep_layer(w_in, w_uq, w_ukv, norm_g, q_lat_g, kv_lat_g, mla_q_g, mla_k_g,
                na_q_g, na_k_g, diff_q_g, diff_k_g):
    sizes = (Q_LORA, KV_LORA, MLA_ROPE, 3 * NA_WIDTH, 3 * DIFF_WIDTH, D_MODEL)
    splits = np.cumsum(sizes)[:-1].tolist()
    w_cq, w_ckv, w_kpe, w_na, w_diff, w_gate = jnp.split(w_in, splits, axis=-1)
    hr = MLA_ROPE // 2
    hd = DIFF_QK // 2
    w_qd, w_kd = w_diff[:, :DIFF_WIDTH], w_diff[:, DIFF_WIDTH:2 * DIFF_WIDTH]
    rot_d = lambda w: _rot_cols(w.reshape(D_MODEL, -1, DIFF_QK), hd).reshape(D_MODEL, DIFF_WIDTH)
    w_ext = jnp.concatenate([
        w_cq, w_ckv,
        _pad_cols(w_kpe, MLA_NOPE, LANES), _pad_cols(_rot_cols(w_kpe, hr), MLA_NOPE, LANES),
        w_na, w_diff, rot_d(w_qd), rot_d(w_kd), w_gate], axis=-1).astype(BF16)

    uq = w_uq.reshape(Q_LORA, MLA_HEADS, MLA_QK)
    uq_raw = jnp.pad(uq, ((0, 0), (0, 0), (0, LANES - MLA_QK)))
    uq_rot = jnp.pad(_rot_cols(uq[..., MLA_NOPE:], hr), ((0, 0), (0, 0), (MLA_NOPE, LANES - MLA_QK)))
    wuq = jnp.concatenate([uq_raw.reshape(Q_LORA, -1), uq_rot.reshape(Q_LORA, -1)], axis=-1).astype(BF16)

    ukv = w_ukv.reshape(KV_LORA, MLA_HEADS, MLA_NOPE + MLA_V)
    uk = jnp.pad(ukv[..., :MLA_NOPE], ((0, 0), (0, 0), (0, LANES - MLA_NOPE)))
    wukv = jnp.concatenate([uk.reshape(KV_LORA, -1), ukv[..., MLA_NOPE:].reshape(KV_LORA, -1)],
                           axis=-1).astype(BF16)

    def row(v):
        return jnp.pad(v.astype(F32), (0, D_MODEL - v.shape[0]))

    def mla_rows(g):
        rot = jnp.pad(_swap_halves(g[MLA_NOPE:], hr), (MLA_NOPE, 0))
        return row(g), row(rot)

    def diff_rows(g):
        n = DIFF_WIDTH // DIFF_QK
        return row(jnp.tile(g, n)), row(jnp.tile(_swap_halves(g, hd), n))

    rows = [row(norm_g), row(q_lat_g), row(kv_lat_g), *mla_rows(mla_q_g), *mla_rows(mla_k_g),
            row(jnp.tile(na_q_g, NA_HEADS)), row(jnp.tile(na_k_g, NA_HEADS)),
            *diff_rows(diff_q_g), *diff_rows(diff_k_g)]
    rows += [jnp.zeros((D_MODEL,), F32)] * (G_ROWS - len(rows))
    return w_ext, wuq, wukv, jnp.stack(rows)


def _rope_tables(S):
    def cs(dim):
        inv = ROPE_THETA ** (-jnp.arange(0, dim, 2, dtype=F32) / dim)
        ang = jnp.arange(S, dtype=F32)[:, None] * inv[None, :]
        return jnp.cos(ang), jnp.sin(ang)

    c, s = cs(MLA_ROPE)
    pad = LANES - MLA_QK
    cm = jnp.concatenate([jnp.ones((S, MLA_NOPE), F32), c, c, jnp.zeros((S, pad), F32)], axis=-1)
    sm = jnp.concatenate([jnp.zeros((S, MLA_NOPE), F32), s, s, jnp.zeros((S, pad), F32)], axis=-1)
    c, s = cs(DIFF_QK)
    n = LANES // DIFF_QK
    cd = jnp.tile(jnp.concatenate([c, c], axis=-1), (1, n))
    sd = jnp.tile(jnp.concatenate([s, s], axis=-1), (1, n))
    return cm, sm, cd, sd


def _tile(S, pref):
    t = pref
    while S % t:
        t //= 2
    return t


def kernel(x, c, ada_w, ada_b, norm_g, w_in, q_lat_g, w_uq, kv_lat_g, w_ukv, mla_q_g, mla_k_g,
           na_q_g, na_k_g, na_rpb, diff_q_g, diff_k_g, lam_q1, lam_k1, lam_q2, lam_k2, subln_g, w_out):
    B, S, D = x.shape
    L = ada_w.shape[0]
    rows = S // GRID_W
    assert D == D_MODEL and S % (NA_RB * GRID_W) == 0 and rows >= NA_KROWS

    tabs = _rope_tables(S)
    mod = _adaln(c, ada_w, ada_b)
    tm = _tile(S, 256)
    tq = _tile(S, 256)
    tk = _tile(S, 512)

    h = x
    for i in range(L):
        lam_init = 0.8 - 0.6 * math.exp(-0.3 * i)
        w_ext, wuq, wukv, gv = _prep_layer(w_in[i], w_uq[i], w_ukv[i], norm_g[i], q_lat_g[i],
                                           kv_lat_g[i], mla_q_g[i], mla_k_g[i], na_q_g[i], na_k_g[i],
                                           diff_q_g[i], diff_k_g[i])
        mod_b = mod[i].transpose(1, 0, 2)
        qm, km, vm, nq, nk, nv, dq, dk, dv, gt = _inproj(h, mod_b, gv, w_ext, wuq, wukv, tabs, tm)
        om = _mla_attn(qm, km, vm, tq, tk)
        on = _na_attn(nq, nk, nv, _na_bias_table(na_rpb[i], rows))
        lamv = jnp.stack([lam_q1[i], lam_k1[i], lam_q2[i], lam_k2[i]]).astype(F32)
        sg = jnp.tile(subln_g[i].astype(F32), LANES // DIFF_V)[None, :]
        od = _diff_attn(dq, dk, dv, lamv, sg, lam_init, tq, tk)
        h = _outproj(h, om, on, od, gt, mod_b, w_out[i].astype(BF16), tm)
    return h
```

```python
import functools
import math

import numpy as np
import jax
import jax.numpy as jnp
from jax import lax
from jax.experimental import pallas as pl
from jax.experimental.pallas import tpu as pltpu

F32 = jnp.float32
BF16 = jnp.bfloat16

D_MODEL = 1024
GRID_W = 64
EPS = 1e-6
ROPE_THETA = 10000.0
MLA_HEADS, MLA_NOPE, MLA_ROPE, MLA_V = 6, 64, 32, 64
MLA_QK = MLA_NOPE + MLA_ROPE
Q_LORA, KV_LORA = 256, 128
NA_HEADS, NA_DIM, NA_KR, NA_KC = 6, 64, 8, 16
DIFF_HEADS, DIFF_QK = 4, 32
DIFF_V = 2 * DIFF_QK
MLA_WIDTH, NA_WIDTH, DIFF_WIDTH = MLA_HEADS * MLA_V, NA_HEADS * NA_DIM, DIFF_HEADS * DIFF_V
HEAD_V = 64

LANES = 128
BF16_SUBLANES = 16
VMEM_LIMIT = 52 * 1024 * 1024

C_CQ, C_CKV, C_KPE, C_KPER = 0, 256, 384, 512
C_NA = 640
C_DIFF = 1792
C_GATE = 2816
N_EXT = 3840

G_NORM, G_QLAT, G_KVLAT, G_MQ, G_MQR, G_MK, G_MKR, G_NQ, G_NK, G_DQ, G_DQR, G_DK, G_DKR = range(13)
G_ROWS = 16

NA_RB = 4
NA_KROWS = 12
NEG = -1e30
LOG2E = math.log2(math.e)


def _nt(a, b):
    return lax.dot_general(a, b, (((1,), (1,)), ((), ())), preferred_element_type=F32)


def _mm(a, b):
    return jnp.dot(a, b, preferred_element_type=F32)


def _seg_ones(seg):
    sh = int(math.log2(seg))
    r = lax.broadcasted_iota(jnp.int32, (LANES, LANES), 0) >> sh
    c = lax.broadcasted_iota(jnp.int32, (LANES, LANES), 1) >> sh
    return jnp.where(r == c, 1.0, 0.0).astype(BF16)


def _seg_sumsq(x, ones):
    x2 = x * x
    hi = x2.astype(BF16)
    lo = (x2 - hi.astype(F32)).astype(BF16)
    return _mm(hi, ones) + _mm(lo, ones)


def _resident(a):
    zeros = (0,) * a.ndim
    return pl.BlockSpec(a.shape, lambda *_: zeros, pipeline_mode=pl.Buffered(1))


def _adaln_body(c_ref, w_ref, b_ref, o_ref):
    c = c_ref[...]
    a = c / (1.0 + jnp.exp(-c))
    o_ref[0, 0] = jnp.dot(a, w_ref[0], preferred_element_type=F32,
                          precision=lax.Precision.HIGHEST) + b_ref[0, 0]


def _adaln(c, ada_w, ada_b):
    L = ada_w.shape[0]
    B = c.shape[0]
    b4 = ada_b.reshape(L, 3, 1, D_MODEL)
    return pl.pallas_call(
        _adaln_body,
        out_shape=jax.ShapeDtypeStruct((L, 3, B, D_MODEL), F32),
        grid=(L, 3),
        in_specs=[
            pl.BlockSpec((B, D_MODEL), lambda l, j: (0, 0)),
            pl.BlockSpec((1, D_MODEL, D_MODEL), lambda l, j: (l, 0, j)),
            pl.BlockSpec((1, 1, 1, D_MODEL), lambda l, j: (l, j, 0, 0)),
        ],
        out_specs=pl.BlockSpec((1, 1, B, D_MODEL), lambda l, j: (l, j, 0, 0)),
        compiler_params=pltpu.CompilerParams(
            dimension_semantics=("arbitrary", "arbitrary"), vmem_limit_bytes=VMEM_LIMIT),
        name="adaln",
    )(c, ada_w, b4)


def _inproj_body(x_ref, mod_ref, gv_ref, w_ref, wuq_ref, wuk_ref, wuvt_ref, wdvt_ref,
                 cm_ref, sm_ref, cd_ref, sd_ref,
                 qm_ref, km_ref, vmt_ref, nq_ref, nk_ref, nv_ref, dq_ref, dk_ref, dvt_ref, gt_ref):
    def gain(row, width, off=0):
        return gv_ref[row:row + 1, off:off + width]

    x = x_ref[0]
    ms = jnp.mean(x * x, axis=-1, keepdims=True)
    xn = x * lax.rsqrt(ms + EPS) * gain(G_NORM, D_MODEL)
    h = (xn * (1.0 + mod_ref[0, 1:2, :]) + mod_ref[0, 0:1, :]).astype(BF16)

    def proj(a, b):
        return _mm(h, w_ref[:, a:b])

    def lat_norm(v, row, width):
        r = lax.rsqrt(jnp.mean(v * v, axis=-1, keepdims=True) + EPS)
        return (v * r * gain(row, width)).astype(BF16)

    cm, sm = cm_ref[...], sm_ref[...]
    cd, sd = cd_ref[...], sd_ref[...]

    cqn = lat_norm(proj(C_CQ, C_CQ + Q_LORA), G_QLAT, Q_LORA)
    qq = _mm(cqn, wuq_ref[...])
    wq_c = gain(G_MQ, LANES) * cm
    wq_s = gain(G_MQR, LANES) * sm
    q_scale = MLA_QK ** -0.5 * LOG2E
    for hd in range(MLA_HEADS):
        a = qq[:, hd * LANES:(hd + 1) * LANES]
        b = qq[:, (MLA_HEADS + hd) * LANES:(MLA_HEADS + hd + 1) * LANES]
        r = lax.rsqrt(jnp.sum(a * a, axis=-1, keepdims=True) * (1.0 / MLA_QK) + EPS)
        qm_ref[0, :, hd * LANES:(hd + 1) * LANES] = ((r * q_scale) * (a * wq_c + b * wq_s)).astype(BF16)

    ckvn = lat_norm(proj(C_CKV, C_CKV + KV_LORA), G_KVLAT, KV_LORA)
    vmt_ref[0, 0] = _nt(wuvt_ref[...], ckvn).astype(BF16)
    kn = _mm(ckvn, wuk_ref[...])
    kpe2 = proj(C_KPE, C_KPE + 2 * LANES)
    kpe, kper = kpe2[:, :LANES], kpe2[:, LANES:]
    wk_c = gain(G_MK, LANES) * cm
    k_rot = kper * (gain(G_MKR, LANES) * sm)
    for hd in range(MLA_HEADS):
        a = kn[:, hd * LANES:(hd + 1) * LANES] + kpe
        r = lax.rsqrt(jnp.sum(a * a, axis=-1, keepdims=True) * (1.0 / MLA_QK) + EPS)
        km_ref[0, :, hd * LANES:(hd + 1) * LANES] = (r * (a * wk_c + k_rot)).astype(BF16)

    nqkv = proj(C_NA, C_NA + 3 * NA_WIDTH)
    ones64 = _seg_ones(NA_DIM)
    na_scale = NA_DIM ** -0.5
    for blk in range(NA_WIDTH // LANES):
        sl = slice(blk * LANES, (blk + 1) * LANES)
        a = nqkv[:, sl]
        r = lax.rsqrt(_seg_sumsq(a, ones64) * (1.0 / NA_DIM) + EPS)
        nq_ref[0, :, sl] = (a * (r * na_scale) * gain(G_NQ, LANES, blk * LANES)).astype(BF16)
        a = nqkv[:, NA_WIDTH + blk * LANES:NA_WIDTH + (blk + 1) * LANES]
        r = lax.rsqrt(_seg_sumsq(a, ones64) * (1.0 / NA_DIM) + EPS)
        nk_ref[0, :, sl] = (a * r * gain(G_NK, LANES, blk * LANES)).astype(BF16)
    nv_ref[0] = nqkv[:, 2 * NA_WIDTH:].astype(BF16)

    dd = proj(C_DIFF, C_DIFF + 4 * DIFF_WIDTH)
    ones32 = _seg_ones(DIFF_QK)
    d_scale = DIFF_QK ** -0.5 * LOG2E
    for blk in range(DIFF_WIDTH // LANES):
        sl = slice(blk * LANES, (blk + 1) * LANES)
        for (o_ref, base, rot, grow, grot, sc) in (
                (dq_ref, 0, 2 * DIFF_WIDTH, G_DQ, G_DQR, d_scale),
                (dk_ref, DIFF_WIDTH, 3 * DIFF_WIDTH, G_DK, G_DKR, 1.0)):
            a = dd[:, base + blk * LANES:base + (blk + 1) * LANES]
            b = dd[:, rot + blk * LANES:rot + (blk + 1) * LANES]
            r = lax.rsqrt(_seg_sumsq(a, ones32) * (1.0 / DIFF_QK) + EPS)
            o_ref[0, :, sl] = ((r * sc) * (a * (gain(grow, LANES, blk * LANES) * cd)
                                           + b * (gain(grot, LANES, blk * LANES) * sd))).astype(BF16)
    dvt_ref[0, 0] = _nt(wdvt_ref[...], h).astype(BF16)

    g = proj(C_GATE, N_EXT)
    gt_ref[0] = (g / (1.0 + jnp.exp(-g))).astype(BF16)


def _inproj(x, mod_b, gv, w_ext, wuq, wuk, wuvt, wdvt, tabs, tm):
    B, S, _ = x.shape
    nt = S // tm
    cm, sm, cd, sd = tabs
    tok = lambda w: pl.BlockSpec((1, tm, w), lambda b, i: (b, i, 0))
    tr = lambda w: pl.BlockSpec((1, 1, w, tm), lambda b, i: (b, i, 0, 0))
    tab = pl.BlockSpec((tm, LANES), lambda b, i: (i, 0))
    tok_out = lambda w: jax.ShapeDtypeStruct((B, S, w), BF16)
    tr_out = lambda w: jax.ShapeDtypeStruct((B, nt, w, tm), BF16)
    weights = (gv, w_ext, wuq, wuk, wuvt, wdvt)
    return pl.pallas_call(
        _inproj_body,
        out_shape=[tok_out(MLA_HEADS * LANES), tok_out(MLA_HEADS * LANES), tr_out(MLA_WIDTH),
                   tok_out(NA_WIDTH), tok_out(NA_WIDTH), tok_out(NA_WIDTH),
                   tok_out(DIFF_WIDTH), tok_out(DIFF_WIDTH), tr_out(DIFF_WIDTH), tok_out(D_MODEL)],
        grid=(B, nt),
        in_specs=[tok(D_MODEL), pl.BlockSpec((1, 3, D_MODEL), lambda b, i: (b, 0, 0)),
                  *[_resident(a) for a in weights], tab, tab, tab, tab],
        out_specs=[tok(MLA_HEADS * LANES), tok(MLA_HEADS * LANES), tr(MLA_WIDTH),
                   tok(NA_WIDTH), tok(NA_WIDTH), tok(NA_WIDTH),
                   tok(DIFF_WIDTH), tok(DIFF_WIDTH), tr(DIFF_WIDTH), tok(D_MODEL)],
        compiler_params=pltpu.CompilerParams(
            dimension_semantics=("parallel", "parallel"), vmem_limit_bytes=VMEM_LIMIT),
        name="inproj",
    )(x, mod_b, *weights, cm, sm, cd, sd)


def _flash_chains(k_ref, vt_ref, qs, kblk, vhead, tq, nk):
    tk = vt_ref.shape[3]
    n = len(qs)

    def step(j, carry):
        off = pl.multiple_of(j * tk, tk)
        k = k_ref[0, pl.ds(off, tk), :]
        vt = vt_ref[0, j]
        ones = jnp.ones((BF16_SUBLANES, tk), BF16)
        vts = {h: jnp.concatenate([vt[h * HEAD_V:(h + 1) * HEAD_V], ones], axis=0) for h in set(vhead)}
        ss = [_nt(k[:, kblk[g] * LANES:(kblk[g] + 1) * LANES], qs[g]) for g in range(n)]
        out = []
        for g in range(n):
            m, acc = carry[g]
            mn = jnp.maximum(m, jnp.max(ss[g], axis=0, keepdims=True))
            p = jnp.exp2(ss[g] - mn).astype(BF16)
            out.append((mn, acc * jnp.exp2(m - mn) + _mm(vts[vhead[g]], p)))
        return tuple(out)

    init = (jnp.full((1, tq), NEG, F32), jnp.zeros((HEAD_V + BF16_SUBLANES, tq), F32))
    res = lax.fori_loop(0, nk, step, (init,) * n)
    return [acc for _, acc in res]


def _normalised(acc):
    return acc[:HEAD_V] / acc[HEAD_V:HEAD_V + 1]


def _mla_body(q_ref, k_ref, vt_ref, o_ref, *, nk):
    tq = q_ref.shape[1]
    heads = list(range(MLA_HEADS))
    qs = [q_ref[0, :, h * LANES:(h + 1) * LANES] for h in heads]
    accs = _flash_chains(k_ref, vt_ref, qs, heads, heads, tq, nk)
    o_ref[0] = jnp.concatenate([_normalised(a) for a in accs], axis=0).T.astype(o_ref.dtype)


def _dense_attn_call(body, name, q, k, vt, extra, out_width, tq):
    B, S, wq = q.shape
    nk, wv, tk = vt.shape[1:]
    return pl.pallas_call(
        body,
        out_shape=jax.ShapeDtypeStruct((B, S, out_width), BF16),
        grid=(B, S // tq),
        in_specs=[pl.BlockSpec((1, tq, wq), lambda b, i: (b, i, 0)),
                  pl.BlockSpec((1, S, wq), lambda b, i: (b, 0, 0)),
                  pl.BlockSpec((1, nk, wv, tk), lambda b, i: (b, 0, 0, 0)),
                  *[pl.BlockSpec(a.shape, lambda b, i: (0, 0)) for a in extra]],
        out_specs=pl.BlockSpec((1, tq, out_width), lambda b, i: (b, i, 0)),
        compiler_params=pltpu.CompilerParams(
            dimension_semantics=("parallel", "arbitrary"), vmem_limit_bytes=VMEM_LIMIT),
        name=name,
    )(q, k, vt, *extra)


def _mla_attn(q, k, vt, tq):
    return _dense_attn_call(functools.partial(_mla_body, nk=vt.shape[1]), "mla_attn", q, k, vt, (), MLA_WIDTH, tq)


def _diff_body(q_ref, k_ref, vt_ref, lam_ref, sg_ref, o_ref, *, nk, lam_init):
    tq = q_ref.shape[1]
    lane = lax.broadcasted_iota(jnp.int32, (1, LANES), 1)
    per_blk = LANES // DIFF_QK
    chains = list(range(2 * DIFF_HEADS))
    qs = []
    for g in chains:
        qb = q_ref[0, :, (g // per_blk) * LANES:(g // per_blk + 1) * LANES]
        lo = (g % per_blk) * DIFF_QK
        qs.append(jnp.where((lane >= lo) & (lane < lo + DIFF_QK), qb, jnp.zeros_like(qb)))
    accs = _flash_chains(k_ref, vt_ref, qs, [g // per_blk for g in chains], [g // 2 for g in chains], tq, nk)

    lv = lam_ref[...]
    lam = (jnp.exp(jnp.sum(lv[0:1] * lv[1:2], axis=-1, keepdims=True))
           - jnp.exp(jnp.sum(lv[2:3] * lv[3:4], axis=-1, keepdims=True)) + lam_init)
    outs = []
    for h in range(DIFF_HEADS):
        o = _normalised(accs[2 * h]) - lam * _normalised(accs[2 * h + 1])
        r = lax.rsqrt(jnp.sum(o * o, axis=0, keepdims=True) * (1.0 / DIFF_V) + EPS)
        outs.append(o * r * (sg_ref[...] * (1.0 - lam_init)))
    o_ref[0] = jnp.concatenate(outs, axis=0).T.astype(o_ref.dtype)


def _diff_attn(q, k, vt, lamv, sgt, lam_init, tq):
    body = functools.partial(_diff_body, nk=vt.shape[1], lam_init=lam_init)
    return _dense_attn_call(body, "diff_attn", q, k, vt, (lamv, sgt), DIFF_WIDTH, tq)


def _na_body(q_ref, k_ref, v_ref, bm_ref, o_ref, *, nrb):
    rb = pl.program_id(2)
    variant = jnp.where(rb == 0, 0, jnp.where(rb == nrb - 1, 2, 1))
    start = NA_RB * jnp.clip(rb - 1, 0, nrb - 3)
    off = pl.multiple_of(start * GRID_W, GRID_W)
    nkeys = NA_KROWS * GRID_W
    q = q_ref[0]
    k = k_ref[0, pl.ds(off, nkeys), :]
    v = v_ref[0, pl.ds(off, nkeys), :]
    lo = lax.broadcasted_iota(jnp.int32, (1, LANES), 1) < NA_DIM
    zq, zv = jnp.zeros_like(q), jnp.zeros_like(v)
    out = None
    for hh in range(2):
        sel = lo if hh == 0 else jnp.logical_not(lo)
        s = _nt(jnp.where(sel, q, zq), k) + bm_ref[hh, variant]
        m = jnp.max(s, axis=-1, keepdims=True)
        p = jnp.exp(s - m)
        l = jnp.sum(p, axis=-1, keepdims=True)
        pv = _mm(p.astype(BF16), jnp.where(sel, v, zv)) * (1.0 / l)
        out = pv if out is None else out + pv
    o_ref[0] = out.astype(o_ref.dtype)


def _na_attn(q, k, v, bm):
    B, S, _ = q.shape
    nrb = S // (NA_RB * GRID_W)
    tq = NA_RB * GRID_W
    return pl.pallas_call(
        functools.partial(_na_body, nrb=nrb),
        out_shape=jax.ShapeDtypeStruct((B, S, NA_WIDTH), BF16),
        grid=(B, NA_HEADS // 2, nrb),
        in_specs=[pl.BlockSpec((1, tq, LANES), lambda b, p, i: (b, i, p)),
                  pl.BlockSpec((1, S, LANES), lambda b, p, i: (b, 0, p)),
                  pl.BlockSpec((1, S, LANES), lambda b, p, i: (b, 0, p)),
                  pl.BlockSpec((2, 3, tq, NA_KROWS * GRID_W), lambda b, p, i: (p, 0, 0, 0))],
        out_specs=pl.BlockSpec((1, tq, LANES), lambda b, p, i: (b, i, p)),
        compiler_params=pltpu.CompilerParams(
            dimension_semantics=("parallel", "parallel", "arbitrary"), vmem_limit_bytes=VMEM_LIMIT),
        name="na_attn",
    )(q, k, v, bm)


def _na_bias_table(rpb, rows):
    nrb = rows // NA_RB
    W = GRID_W
    H = rpb.shape[0]
    pad_c = W - NA_KC
    rp = jnp.pad(rpb.astype(F32), ((0, 0), (0, 0), (pad_c, pad_c)))
    t = jnp.stack([rp[:, :, W - 1 - c:2 * W - 1 - c] for c in range(W)], axis=1)
    pad_lo, pad_hi = NA_KR // 2, NA_KROWS - NA_KR
    t = jnp.pad(t, ((0, 0), (0, 0), (pad_lo, pad_hi), (0, 0)))
    c = np.arange(W)[:, None, None]
    kri = np.arange(NA_KROWS)[None, :, None]
    kc = np.arange(W)[None, None, :]
    c0 = np.clip(c - NA_KC // 2, 0, W - NA_KC)
    col_ok = (kc >= c0) & (kc < c0 + NA_KC)
    blocks = []
    for rb in (0, 1, nrb - 1):
        start = NA_RB * min(max(rb - 1, 0), nrb - 3)
        for ri in range(NA_RB):
            r = NA_RB * rb + ri
            r0 = min(max(r - NA_KR // 2, 0), rows - NA_KR)
            d0 = start - r + (NA_KR - 1) + pad_lo
            assert 0 <= d0 and d0 + NA_KROWS <= t.shape[2]
            ok = ((start + kri >= r0) & (start + kri < r0 + NA_KR)) & col_ok
            blk = jnp.where(ok[None], t[:, :, d0:d0 + NA_KROWS, :], NEG)
            blocks.append(blk.reshape(H, W, NA_KROWS * W))
    return jnp.stack(blocks, axis=1).reshape(H, 3, NA_RB * W, NA_KROWS * W)


def _outproj_body(x_ref, om_ref, on_ref, od_ref, gt_ref, gate_ref, w_ref, o_ref):
    def branch(o_ref_, a, b):
        mix = (o_ref_[0].astype(F32) * gt_ref[0, :, a:b].astype(F32)).astype(BF16)
        return _mm(mix, w_ref[a:b, :])

    y = (branch(om_ref, 0, MLA_WIDTH) + branch(on_ref, MLA_WIDTH, MLA_WIDTH + NA_WIDTH)
         + branch(od_ref, MLA_WIDTH + NA_WIDTH, D_MODEL))
    o_ref[0] = x_ref[0] + gate_ref[0, 2:3, :] * y


def _outproj(x, om, on, od, gt, mod_b, w_out, tm):
    B, S, _ = x.shape
    tok = lambda w: pl.BlockSpec((1, tm, w), lambda b, i: (b, i, 0))
    return pl.pallas_call(
        _outproj_body,
        out_shape=jax.ShapeDtypeStruct((B, S, D_MODEL), F32),
        grid=(B, S // tm),
        in_specs=[tok(D_MODEL), tok(MLA_WIDTH), tok(NA_WIDTH), tok(DIFF_WIDTH), tok(D_MODEL),
                  pl.BlockSpec((1, 3, D_MODEL), lambda b, i: (b, 0, 0)),
                  _resident(w_out)],
        out_specs=tok(D_MODEL),
        compiler_params=pltpu.CompilerParams(
            dimension_semantics=("parallel", "parallel"), vmem_limit_bytes=VMEM_LIMIT),
        name="outproj",
    )(x, om, on, od, gt, mod_b, w_out)


def _rot_cols(w, half):
    return jnp.concatenate([-w[..., half:], w[..., :half]], axis=-1)


def _swap_halves(g, half):
    return jnp.concatenate([g[..., half:], g[..., :half]], axis=-1)


def _pad_cols(w, left, total):
    return jnp.pad(w, ((0, 0), (left, total - left - w.shape[-1])))


def _prep_layer(w_in, w_uq, w_ukv, norm_g, q_lat_g, kv_lat_g, mla_q_g, mla_k_g,
                na_q_g, na_k_g, diff_q_g, diff_k_g):
    sizes = (Q_LORA, KV_LORA, MLA_ROPE, 3 * NA_WIDTH, 3 * DIFF_WIDTH, D_MODEL)
    splits = np.cumsum(sizes)[:-1].tolist()
    w_cq, w_ckv, w_kpe, w_na, w_diff, w_gate = jnp.split(w_in, splits, axis=-1)
    hr = MLA_ROPE // 2
    hd = DIFF_QK // 2
    w_qd, w_kd, w_vd = jnp.split(w_diff, 3, axis=-1)
    rot_d = lambda w: _rot_cols(w.reshape(D_MODEL, -1, DIFF_QK), hd).reshape(D_MODEL, DIFF_WIDTH)
    w_ext = jnp.concatenate([
        w_cq, w_ckv,
        _pad_cols(w_kpe, MLA_NOPE, LANES), _pad_cols(_rot_cols(w_kpe, hr), MLA_NOPE, LANES),
        w_na, w_qd, w_kd, rot_d(w_qd), rot_d(w_kd), w_gate], axis=-1).astype(BF16)
    wdvt = w_vd.T.astype(BF16)

    uq = w_uq.reshape(Q_LORA, MLA_HEADS, MLA_QK)
    uq_raw = jnp.pad(uq, ((0, 0), (0, 0), (0, LANES - MLA_QK)))
    uq_rot = jnp.pad(_rot_cols(uq[..., MLA_NOPE:], hr), ((0, 0), (0, 0), (MLA_NOPE, LANES - MLA_QK)))
    wuq = jnp.concatenate([uq_raw.reshape(Q_LORA, -1), uq_rot.reshape(Q_LORA, -1)], axis=-1).astype(BF16)

    ukv = w_ukv.reshape(KV_LORA, MLA_HEADS, MLA_NOPE + MLA_V)
    wuk = jnp.pad(ukv[..., :MLA_NOPE], ((0, 0), (0, 0), (0, LANES - MLA_NOPE))).reshape(KV_LORA, -1).astype(BF16)
    wuvt = ukv[..., MLA_NOPE:].reshape(KV_LORA, -1).T.astype(BF16)

    def row(v):
        return jnp.pad(v.astype(F32), (0, D_MODEL - v.shape[0]))

    def mla_rows(g):
        rot = jnp.pad(_swap_halves(g[MLA_NOPE:], hr), (MLA_NOPE, 0))
        return row(g), row(rot)

    def diff_rows(g):
        n = DIFF_WIDTH // DIFF_QK
        return row(jnp.tile(g, n)), row(jnp.tile(_swap_halves(g, hd), n))

    rows = [row(norm_g), row(q_lat_g), row(kv_lat_g), *mla_rows(mla_q_g), *mla_rows(mla_k_g),
            row(jnp.tile(na_q_g, NA_HEADS)), row(jnp.tile(na_k_g, NA_HEADS)),
            *diff_rows(diff_q_g), *diff_rows(diff_k_g)]
    rows += [jnp.zeros((D_MODEL,), F32)] * (G_ROWS - len(rows))
    return w_ext, wuq, wuk, wuvt, wdvt, jnp.stack(rows)


def _rope_tables(S):
    def cs(dim):
        inv = ROPE_THETA ** (-jnp.arange(0, dim, 2, dtype=F32) / dim)
        ang = jnp.arange(S, dtype=F32)[:, None] * inv[None, :]
        return jnp.cos(ang), jnp.sin(ang)

    c, s = cs(MLA_ROPE)
    pad = LANES - MLA_QK
    cm = jnp.concatenate([jnp.ones((S, MLA_NOPE), F32), c, c, jnp.zeros((S, pad), F32)], axis=-1)
    sm = jnp.concatenate([jnp.zeros((S, MLA_NOPE), F32), s, s, jnp.zeros((S, pad), F32)], axis=-1)
    c, s = cs(DIFF_QK)
    n = LANES // DIFF_QK
    cd = jnp.tile(jnp.concatenate([c, c], axis=-1), (1, n))
    sd = jnp.tile(jnp.concatenate([s, s], axis=-1), (1, n))
    return cm, sm, cd, sd


def _tile(S, pref):
    t = pref
    while S % t:
        t //= 2
    return t


def kernel(x, c, ada_w, ada_b, norm_g, w_in, q_lat_g, w_uq, kv_lat_g, w_ukv, mla_q_g, mla_k_g,
           na_q_g, na_k_g, na_rpb, diff_q_g, diff_k_g, lam_q1, lam_k1, lam_q2, lam_k2, subln_g, w_out):
    B, S, D = x.shape
    L = ada_w.shape[0]
    rows = S // GRID_W
    assert D == D_MODEL and S % (NA_RB * GRID_W) == 0 and rows >= NA_KROWS

    tabs = _rope_tables(S)
    mod = _adaln(c, ada_w, ada_b)
    tm = _tile(S, 512)
    tq = _tile(S, 512)

    h = x
    for i in range(L):
        lam_init = 0.8 - 0.6 * math.exp(-0.3 * i)
        w_ext, wuq, wuk, wuvt, wdvt, gv = _prep_layer(
            w_in[i], w_uq[i], w_ukv[i], norm_g[i], q_lat_g[i], kv_lat_g[i], mla_q_g[i], mla_k_g[i],
            na_q_g[i], na_k_g[i], diff_q_g[i], diff_k_g[i])
        mod_b = mod[i].transpose(1, 0, 2)
        qm, km, vmt, nq, nk, nv, dq, dk, dvt, gt = _inproj(h, mod_b, gv, w_ext, wuq, wuk, wuvt, wdvt, tabs, tm)
        om = _mla_attn(qm, km, vmt, tq)
        on = _na_attn(nq, nk, nv, _na_bias_table(na_rpb[i], rows))
        lamv = jnp.stack([lam_q1[i], lam_k1[i], lam_q2[i], lam_k2[i]]).astype(F32)
        sgt = jnp.broadcast_to(subln_g[i].astype(F32)[:, None], (DIFF_V, tq))
        od = _diff_attn(dq, dk, dvt, lamv, sgt, lam_init, tq)
        h = _outproj(h, om, on, od, gt, mod_b, w_out[i].astype(BF16), tm)
    return h
```

```python
import functools
import math

import numpy as np
import jax
import jax.numpy as jnp
from jax import lax
from jax.experimental import pallas as pl
from jax.experimental.pallas import tpu as pltpu

F32 = jnp.float32
BF16 = jnp.bfloat16

D_MODEL = 1024
GRID_W = 64
EPS = 1e-6
ROPE_THETA = 10000.0
MLA_HEADS, MLA_NOPE, MLA_ROPE, MLA_V = 6, 64, 32, 64
MLA_QK = MLA_NOPE + MLA_ROPE
Q_LORA, KV_LORA = 256, 128
NA_HEADS, NA_DIM, NA_KR, NA_KC = 6, 64, 8, 16
DIFF_HEADS, DIFF_QK = 4, 32
DIFF_V = 2 * DIFF_QK
MLA_WIDTH, NA_WIDTH, DIFF_WIDTH = MLA_HEADS * MLA_V, NA_HEADS * NA_DIM, DIFF_HEADS * DIFF_V
HEAD_V = 64

LANES = 128
BF16_SUBLANES = 16
MXU_N = 256
VMEM_LIMIT = 52 * 1024 * 1024

C_CQ, C_CKV, C_KPE, C_KPER = 0, 256, 384, 512
C_NA = 640
C_DIFF = 1408
C_GATE = 2432
N_EXT = 3456

G_NORM, G_QLAT, G_KVLAT, G_MQ, G_MQR, G_MK, G_MKR, G_NQ, G_NK, G_DQ, G_DQR, G_DK, G_DKR = range(13)
G_ROWS = 16

TILE_M = 512
TILE_Q_MLA = 1024
TILE_Q_DIFF = 512
TILE_K = 512
QK_AHEAD = 3

NA_RB = 4
NA_KROWS = 12
NEG = -1e30
LOG2E = math.log2(math.e)


def _nt(a, b):
    return lax.dot_general(a, b, (((1,), (1,)), ((), ())), preferred_element_type=F32)


def _mm(a, b):
    return jnp.dot(a, b, preferred_element_type=F32)


def _seg_ones(seg, width):
    sh = int(math.log2(seg))
    r = lax.broadcasted_iota(jnp.int32, (width, width), 0) >> sh
    c = lax.broadcasted_iota(jnp.int32, (width, width), 1) >> sh
    return jnp.where(r == c, 1.0, 0.0).astype(BF16)


def _seg_rms_scale(x, seg):
    x2 = (x * x).astype(BF16)
    width = x.shape[1]
    parts = []
    for a in range(0, width, MXU_N):
        w = min(MXU_N, width - a)
        parts.append(_mm(x2[:, a:a + w], _seg_ones(seg, w)))
    ss = parts[0] if len(parts) == 1 else jnp.concatenate(parts, axis=1)
    return lax.rsqrt(ss * (1.0 / seg) + EPS)


def _resident(a):
    zeros = (0,) * a.ndim
    return pl.BlockSpec(a.shape, lambda *_: zeros, pipeline_mode=pl.Buffered(1))


def _adaln_body(c_ref, w_ref, b_ref, o_ref):
    c = c_ref[...]
    a = c / (1.0 + jnp.exp(-c))
    o_ref[0, 0] = jnp.dot(a, w_ref[0], preferred_element_type=F32,
                          precision=lax.Precision.HIGHEST) + b_ref[0, 0]


def _adaln(c, ada_w, ada_b):
    L = ada_w.shape[0]
    B = c.shape[0]
    b4 = ada_b.reshape(L, 3, 1, D_MODEL)
    return pl.pallas_call(
        _adaln_body,
        out_shape=jax.ShapeDtypeStruct((L, 3, B, D_MODEL), F32),
        grid=(L, 3),
        in_specs=[
            pl.BlockSpec((B, D_MODEL), lambda l, j: (0, 0)),
            pl.BlockSpec((1, D_MODEL, D_MODEL), lambda l, j: (l, 0, j)),
            pl.BlockSpec((1, 1, 1, D_MODEL), lambda l, j: (l, j, 0, 0)),
        ],
        out_specs=pl.BlockSpec((1, 1, B, D_MODEL), lambda l, j: (l, j, 0, 0)),
        compiler_params=pltpu.CompilerParams(
            dimension_semantics=("arbitrary", "arbitrary"), vmem_limit_bytes=VMEM_LIMIT),
        name="adaln",
    )(c, ada_w, b4)


def _inproj_body(x_ref, mod_ref, gv_ref, w_ref, wuq_ref, wuk_ref, wuvt_ref, wnvt_ref, wdvt_ref,
                 cm_ref, sm_ref, cd_ref, sd_ref,
                 qm_ref, km_ref, vmt_ref, nq_ref, nk_ref, nvt_ref, dq_ref, dk_ref, dvt_ref, gt_ref):
    def gain(row, width, off=0):
        return gv_ref[row:row + 1, off:off + width]

    x = x_ref[0]
    ms = jnp.mean(x * x, axis=-1, keepdims=True)
    xn = x * lax.rsqrt(ms + EPS) * gain(G_NORM, D_MODEL)
    h = (xn * (1.0 + mod_ref[0, 1:2, :]) + mod_ref[0, 0:1, :]).astype(BF16)

    def proj(a, b):
        return _mm(h, w_ref[:, a:b])

    def lat_norm(v, row, width):
        r = lax.rsqrt(jnp.mean(v * v, axis=-1, keepdims=True) + EPS)
        return (v * r * gain(row, width)).astype(BF16)

    cm, sm = cm_ref[...], sm_ref[...]
    cd, sd = cd_ref[...], sd_ref[...]

    cqn = lat_norm(proj(C_CQ, C_CQ + Q_LORA), G_QLAT, Q_LORA)
    qq = _mm(cqn, wuq_ref[...])
    wq_c = gain(G_MQ, LANES) * cm
    wq_s = gain(G_MQR, LANES) * sm
    q_scale = MLA_QK ** -0.5 * LOG2E
    for hd in range(MLA_HEADS):
        a = qq[:, hd * LANES:(hd + 1) * LANES]
        b = qq[:, (MLA_HEADS + hd) * LANES:(MLA_HEADS + hd + 1) * LANES]
        r = lax.rsqrt(jnp.sum(a * a, axis=-1, keepdims=True) * (1.0 / MLA_QK) + EPS)
        qm_ref[0, :, hd * LANES:(hd + 1) * LANES] = ((r * q_scale) * (a * wq_c + b * wq_s)).astype(BF16)

    ckvn = lat_norm(proj(C_CKV, C_CKV + KV_LORA), G_KVLAT, KV_LORA)
    _store_chunks(vmt_ref, _nt(wuvt_ref[...], ckvn).astype(BF16))
    kn = _mm(ckvn, wuk_ref[...])
    kpe2 = proj(C_KPE, C_KPE + 2 * LANES)
    kpe, kper = kpe2[:, :LANES], kpe2[:, LANES:]
    wk_c = gain(G_MK, LANES) * cm
    k_rot = kper * (gain(G_MKR, LANES) * sm)
    for hd in range(MLA_HEADS):
        a = kn[:, hd * LANES:(hd + 1) * LANES] + kpe
        r = lax.rsqrt(jnp.sum(a * a, axis=-1, keepdims=True) * (1.0 / MLA_QK) + EPS)
        km_ref[0, :, hd * LANES:(hd + 1) * LANES] = (r * (a * wk_c + k_rot)).astype(BF16)

    nqk = proj(C_NA, C_NA + 2 * NA_WIDTH)
    na_scale = NA_DIM ** -0.5 * LOG2E
    a = nqk[:, :NA_WIDTH]
    nq_ref[0] = (a * (_seg_rms_scale(a, NA_DIM) * na_scale) * gain(G_NQ, NA_WIDTH)).astype(BF16)
    a = nqk[:, NA_WIDTH:]
    nk_ref[0] = (a * _seg_rms_scale(a, NA_DIM) * gain(G_NK, NA_WIDTH)).astype(BF16)
    _store_chunks(nvt_ref, _nt(wnvt_ref[...], h).astype(BF16))

    dd = proj(C_DIFF, C_DIFF + 4 * DIFF_WIDTH)
    d_scale = DIFF_QK ** -0.5 * LOG2E
    reps = DIFF_WIDTH // LANES
    cdw, sdw = jnp.concatenate([cd] * reps, axis=1), jnp.concatenate([sd] * reps, axis=1)
    for (o_ref, base, rot, grow, grot, sc) in (
            (dq_ref, 0, 2 * DIFF_WIDTH, G_DQ, G_DQR, d_scale),
            (dk_ref, DIFF_WIDTH, 3 * DIFF_WIDTH, G_DK, G_DKR, 1.0)):
        a = dd[:, base:base + DIFF_WIDTH]
        b = dd[:, rot:rot + DIFF_WIDTH]
        r = _seg_rms_scale(a, DIFF_QK)
        o_ref[0] = ((r * sc) * (a * (gain(grow, DIFF_WIDTH) * cdw) + b * (gain(grot, DIFF_WIDTH) * sdw))).astype(BF16)
    _store_chunks(dvt_ref, _nt(wdvt_ref[...], h).astype(BF16))

    g = proj(C_GATE, N_EXT)
    gt_ref[0] = (g / (1.0 + jnp.exp(-g))).astype(BF16)


def _store_chunks(ref, val):
    tk = ref.shape[3]
    for c in range(ref.shape[1]):
        ref[0, c] = val[:, c * tk:(c + 1) * tk]


def _inproj(x, mod_b, gv, w_ext, wuq, wuk, wuvt, wnvt, wdvt, tabs, tm, tk):
    B, S, _ = x.shape
    nt = S // tm
    cm, sm, cd, sd = tabs
    tok = lambda w: pl.BlockSpec((1, tm, w), lambda b, i: (b, i, 0))
    tr = lambda w, c: pl.BlockSpec((1, tm // c, w, c), lambda b, i: (b, i, 0, 0))
    tab = pl.BlockSpec((tm, LANES), lambda b, i: (i, 0))
    tok_out = lambda w: jax.ShapeDtypeStruct((B, S, w), BF16)
    tr_out = lambda w, c: jax.ShapeDtypeStruct((B, S // c, w, c), BF16)
    tna = NA_RB * GRID_W
    weights = (gv, w_ext, wuq, wuk, wuvt, wnvt, wdvt)
    return pl.pallas_call(
        _inproj_body,
        out_shape=[tok_out(MLA_HEADS * LANES), tok_out(MLA_HEADS * LANES), tr_out(MLA_WIDTH, tk),
                   tok_out(NA_WIDTH), tok_out(NA_WIDTH), tr_out(NA_WIDTH, tna),
                   tok_out(DIFF_WIDTH), tok_out(DIFF_WIDTH), tr_out(DIFF_WIDTH, tk), tok_out(D_MODEL)],
        grid=(B, nt),
        in_specs=[tok(D_MODEL), pl.BlockSpec((1, 3, D_MODEL), lambda b, i: (b, 0, 0)),
                  *[_resident(a) for a in weights], tab, tab, tab, tab],
        out_specs=[tok(MLA_HEADS * LANES), tok(MLA_HEADS * LANES), tr(MLA_WIDTH, tk),
                   tok(NA_WIDTH), tok(NA_WIDTH), tr(NA_WIDTH, tna),
                   tok(DIFF_WIDTH), tok(DIFF_WIDTH), tr(DIFF_WIDTH, tk), tok(D_MODEL)],
        compiler_params=pltpu.CompilerParams(
            dimension_semantics=("parallel", "parallel"), vmem_limit_bytes=VMEM_LIMIT),
        name="inproj",
    )(x, mod_b, *weights, cm, sm, cd, sd)


def _flash_chains(k_ref, vt_ref, qs, kblk, vhead, tq, nk):
    tk = vt_ref.shape[3]
    n = len(qs)

    def step(j, carry):
        off = pl.multiple_of(j * tk, tk)
        k = k_ref[0, pl.ds(off, tk), :]
        vt = vt_ref[0, j]
        ones = jnp.ones((BF16_SUBLANES, tk), BF16)
        vts = {h: jnp.concatenate([vt[h * HEAD_V:(h + 1) * HEAD_V], ones], axis=0) for h in set(vhead)}
        def scores(g):
            return _nt(k[:, kblk[g] * LANES:(kblk[g] + 1) * LANES], qs[g])

        pending = [scores(g) for g in range(min(QK_AHEAD, n))]
        out = []
        for g in range(n):
            s = pending.pop(0)
            if g + QK_AHEAD < n:
                pending.append(scores(g + QK_AHEAD))
            m, acc = carry[g]
            mn = jnp.maximum(m, jnp.max(s, axis=0, keepdims=True))
            p = jnp.exp2(s - mn).astype(BF16)
            out.append((mn, acc * jnp.exp2(m - mn) + _mm(vts[vhead[g]], p)))
        return tuple(out)

    init = (jnp.full((1, tq), NEG, F32), jnp.zeros((HEAD_V + BF16_SUBLANES, tq), F32))
    res = lax.fori_loop(0, nk, step, (init,) * n)
    return [acc for _, acc in res]


def _normalised(acc):
    return acc[:HEAD_V] / acc[HEAD_V:HEAD_V + 1]


def _mla_body(q_ref, k_ref, vt_ref, o_ref, *, nk):
    tq = q_ref.shape[1]
    heads = list(range(MLA_HEADS))
    qs = [q_ref[0, :, h * LANES:(h + 1) * LANES] for h in heads]
    accs = _flash_chains(k_ref, vt_ref, qs, heads, heads, tq, nk)
    o_ref[0] = jnp.concatenate([_normalised(a) for a in accs], axis=0).T.astype(o_ref.dtype)


def _dense_attn_call(body, name, q, k, vt, extra, out_width, tq):
    B, S, wq = q.shape
    nk, wv, tk = vt.shape[1:]
    return pl.pallas_call(
        body,
        out_shape=jax.ShapeDtypeStruct((B, S, out_width), BF16),
        grid=(B, S // tq),
        in_specs=[pl.BlockSpec((1, tq, wq), lambda b, i: (b, i, 0)),
                  pl.BlockSpec((1, S, wq), lambda b, i: (b, 0, 0)),
                  pl.BlockSpec((1, nk, wv, tk), lambda b, i: (b, 0, 0, 0)),
                  *[pl.BlockSpec(a.shape, lambda b, i: (0, 0)) for a in extra]],
        out_specs=pl.BlockSpec((1, tq, out_width), lambda b, i: (b, i, 0)),
        compiler_params=pltpu.CompilerParams(
            dimension_semantics=("parallel", "arbitrary"), vmem_limit_bytes=VMEM_LIMIT),
        name=name,
    )(q, k, vt, *extra)


def _mla_attn(q, k, vt, tq):
    return _dense_attn_call(functools.partial(_mla_body, nk=vt.shape[1]), "mla_attn", q, k, vt, (), MLA_WIDTH, tq)


def _diff_body(q_ref, k_ref, vt_ref, lam_ref, sg_ref, o_ref, *, nk, lam_init):
    tq = q_ref.shape[1]
    lane = lax.broadcasted_iota(jnp.int32, (1, LANES), 1)
    per_blk = LANES // DIFF_QK
    chains = list(range(2 * DIFF_HEADS))
    qs = []
    for g in chains:
        qb = q_ref[0, :, (g // per_blk) * LANES:(g // per_blk + 1) * LANES]
        lo = (g % per_blk) * DIFF_QK
        qs.append(jnp.where((lane >= lo) & (lane < lo + DIFF_QK), qb, jnp.zeros_like(qb)))
    accs = _flash_chains(k_ref, vt_ref, qs, [g // per_blk for g in chains], [g // 2 for g in chains], tq, nk)

    lv = lam_ref[...]
    lam = (jnp.exp(jnp.sum(lv[0:1] * lv[1:2], axis=-1, keepdims=True))
           - jnp.exp(jnp.sum(lv[2:3] * lv[3:4], axis=-1, keepdims=True)) + lam_init)
    outs = []
    for h in range(DIFF_HEADS):
        o = _normalised(accs[2 * h]) - lam * _normalised(accs[2 * h + 1])
        r = lax.rsqrt(jnp.sum(o * o, axis=0, keepdims=True) * (1.0 / DIFF_V) + EPS)
        outs.append(o * r * (sg_ref[...] * (1.0 - lam_init)))
    o_ref[0] = jnp.concatenate(outs, axis=0).T.astype(o_ref.dtype)


def _diff_attn(q, k, vt, lamv, sgt, lam_init, tq):
    body = functools.partial(_diff_body, nk=vt.shape[1], lam_init=lam_init)
    return _dense_attn_call(body, "diff_attn", q, k, vt, (lamv, sgt), DIFF_WIDTH, tq)


def _na_body(q_ref, k_ref, vt_ref, bm_ref, o_ref, *, nrb):
    rb = pl.program_id(1)
    variant = jnp.where(rb == 0, 0, jnp.where(rb == nrb - 1, 2, 1))
    chunk0 = jnp.clip(rb - 1, 0, nrb - 3)
    tq = NA_RB * GRID_W
    off = pl.multiple_of(chunk0 * tq, tq)
    nkeys = NA_KROWS * GRID_W
    k = k_ref[0, pl.ds(off, nkeys), :]
    vt = jnp.concatenate([vt_ref[0, chunk0 + c] for c in range(NA_KROWS // NA_RB)], axis=1)
    ones = jnp.ones((BF16_SUBLANES, nkeys), BF16)
    lane = lax.broadcasted_iota(jnp.int32, (1, LANES), 1)
    per_blk = LANES // NA_DIM

    def scores(h):
        blk = slice((h // per_blk) * LANES, (h // per_blk + 1) * LANES)
        qb = q_ref[0, :, blk]
        lo = (h % per_blk) * NA_DIM
        qh = jnp.where((lane >= lo) & (lane < lo + NA_DIM), qb, jnp.zeros_like(qb))
        return _nt(k[:, blk], qh) + bm_ref[h, variant]

    pending = [scores(h) for h in range(min(QK_AHEAD, NA_HEADS))]
    outs = []
    for h in range(NA_HEADS):
        s = pending.pop(0)
        if h + QK_AHEAD < NA_HEADS:
            pending.append(scores(h + QK_AHEAD))
        p = jnp.exp2(s - jnp.max(s, axis=0, keepdims=True)).astype(BF16)
        vt_ext = jnp.concatenate([vt[h * NA_DIM:(h + 1) * NA_DIM], ones], axis=0)
        acc = _mm(vt_ext, p)
        outs.append(acc[:NA_DIM] / acc[NA_DIM:NA_DIM + 1])
    o_ref[0] = jnp.concatenate(outs, axis=0).T.astype(o_ref.dtype)


def _na_attn(q, k, vt, bm):
    B, S, _ = q.shape
    tq = NA_RB * GRID_W
    nrb = S // tq
    return pl.pallas_call(
        functools.partial(_na_body, nrb=nrb),
        out_shape=jax.ShapeDtypeStruct((B, S, NA_WIDTH), BF16),
        grid=(B, nrb),
        in_specs=[pl.BlockSpec((1, tq, NA_WIDTH), lambda b, i: (b, i, 0)),
                  pl.BlockSpec((1, S, NA_WIDTH), lambda b, i: (b, 0, 0)),
                  pl.BlockSpec((1, nrb, NA_WIDTH, tq), lambda b, i: (b, 0, 0, 0)),
                  _resident(bm)],
        out_specs=pl.BlockSpec((1, tq, NA_WIDTH), lambda b, i: (b, i, 0)),
        compiler_params=pltpu.CompilerParams(
            dimension_semantics=("parallel", "arbitrary"), vmem_limit_bytes=VMEM_LIMIT),
        name="na_attn",
    )(q, k, vt, bm)


def _na_bias_table(rpb, rows):
    nrb = rows // NA_RB
    W = GRID_W
    H = rpb.shape[0]
    pad_c = W - NA_KC
    rp = jnp.pad(rpb.astype(F32), ((0, 0), (0, 0), (pad_c, pad_c)))
    t = jnp.stack([rp[:, :, W - 1 - c:2 * W - 1 - c] for c in range(W)], axis=1)
    pad_lo, pad_hi = NA_KR // 2, NA_KROWS - NA_KR
    t = jnp.pad(t, ((0, 0), (0, 0), (pad_lo, pad_hi), (0, 0)))
    c = np.arange(W)[:, None, None]
    kri = np.arange(NA_KROWS)[None, :, None]
    kc = np.arange(W)[None, None, :]
    c0 = np.clip(c - NA_KC // 2, 0, W - NA_KC)
    col_ok = (kc >= c0) & (kc < c0 + NA_KC)
    blocks = []
    for rb in (0, 1, nrb - 1):
        start = NA_RB * min(max(rb - 1, 0), nrb - 3)
        for ri in range(NA_RB):
            r = NA_RB * rb + ri
            r0 = min(max(r - NA_KR // 2, 0), rows - NA_KR)
            d0 = start - r + (NA_KR - 1) + pad_lo
            assert 0 <= d0 and d0 + NA_KROWS <= t.shape[2]
            ok = ((start + kri >= r0) & (start + kri < r0 + NA_KR)) & col_ok
            blk = jnp.where(ok[None], t[:, :, d0:d0 + NA_KROWS, :] * LOG2E, NEG)
            blocks.append(blk.reshape(H, W, NA_KROWS * W))
    table = jnp.stack(blocks, axis=1).reshape(H, 3, NA_RB * W, NA_KROWS * W)
    return table.transpose(0, 1, 3, 2)


def _outproj_body(x_ref, om_ref, on_ref, od_ref, gt_ref, gate_ref, w_ref, o_ref):
    def branch(o_ref_, a, b):
        mix = (o_ref_[0].astype(F32) * gt_ref[0, :, a:b].astype(F32)).astype(BF16)
        return _mm(mix, w_ref[a:b, :])

    y = (branch(om_ref, 0, MLA_WIDTH) + branch(on_ref, MLA_WIDTH, MLA_WIDTH + NA_WIDTH)
         + branch(od_ref, MLA_WIDTH + NA_WIDTH, D_MODEL))
    o_ref[0] = x_ref[0] + gate_ref[0, 2:3, :] * y


def _outproj(x, om, on, od, gt, mod_b, w_out, tm):
    B, S, _ = x.shape
    tok = lambda w: pl.BlockSpec((1, tm, w), lambda b, i: (b, i, 0))
    return pl.pallas_call(
        _outproj_body,
        out_shape=jax.ShapeDtypeStruct((B, S, D_MODEL), F32),
        grid=(B, S // tm),
        in_specs=[tok(D_MODEL), tok(MLA_WIDTH), tok(NA_WIDTH), tok(DIFF_WIDTH), tok(D_MODEL),
                  pl.BlockSpec((1, 3, D_MODEL), lambda b, i: (b, 0, 0)),
                  _resident(w_out)],
        out_specs=tok(D_MODEL),
        compiler_params=pltpu.CompilerParams(
            dimension_semantics=("parallel", "parallel"), vmem_limit_bytes=VMEM_LIMIT),
        name="outproj",
    )(x, om, on, od, gt, mod_b, w_out)


def _rot_cols(w, half):
    return jnp.concatenate([-w[..., half:], w[..., :half]], axis=-1)


def _swap_halves(g, half):
    return jnp.concatenate([g[..., half:], g[..., :half]], axis=-1)


def _pad_cols(w, left, total):
    return jnp.pad(w, ((0, 0), (left, total - left - w.shape[-1])))


def _prep_layer(w_in, w_uq, w_ukv, norm_g, q_lat_g, kv_lat_g, mla_q_g, mla_k_g,
                na_q_g, na_k_g, diff_q_g, diff_k_g):
    sizes = (Q_LORA, KV_LORA, MLA_ROPE, 3 * NA_WIDTH, 3 * DIFF_WIDTH, D_MODEL)
    splits = np.cumsum(sizes)[:-1].tolist()
    w_cq, w_ckv, w_kpe, w_na, w_diff, w_gate = jnp.split(w_in, splits, axis=-1)
    hr = MLA_ROPE // 2
    hd = DIFF_QK // 2
    w_qd, w_kd, w_vd = jnp.split(w_diff, 3, axis=-1)
    w_nq, w_nk, w_nv = jnp.split(w_na, 3, axis=-1)
    rot_d = lambda w: _rot_cols(w.reshape(D_MODEL, -1, DIFF_QK), hd).reshape(D_MODEL, DIFF_WIDTH)
    w_ext = jnp.concatenate([
        w_cq, w_ckv,
        _pad_cols(w_kpe, MLA_NOPE, LANES), _pad_cols(_rot_cols(w_kpe, hr), MLA_NOPE, LANES),
        w_nq, w_nk, w_qd, w_kd, rot_d(w_qd), rot_d(w_kd), w_gate], axis=-1).astype(BF16)
    wnvt = w_nv.T.astype(BF16)
    wdvt = w_vd.T.astype(BF16)

    uq = w_uq.reshape(Q_LORA, MLA_HEADS, MLA_QK)
    uq_raw = jnp.pad(uq, ((0, 0), (0, 0), (0, LANES - MLA_QK)))
    uq_rot = jnp.pad(_rot_cols(uq[..., MLA_NOPE:], hr), ((0, 0), (0, 0), (MLA_NOPE, LANES - MLA_QK)))
    wuq = jnp.concatenate([uq_raw.reshape(Q_LORA, -1), uq_rot.reshape(Q_LORA, -1)], axis=-1).astype(BF16)

    ukv = w_ukv.reshape(KV_LORA, MLA_HEADS, MLA_NOPE + MLA_V)
    wuk = jnp.pad(ukv[..., :MLA_NOPE], ((0, 0), (0, 0), (0, LANES - MLA_NOPE))).reshape(KV_LORA, -1).astype(BF16)
    wuvt = ukv[..., MLA_NOPE:].reshape(KV_LORA, -1).T.astype(BF16)

    def row(v):
        return jnp.pad(v.astype(F32), (0, D_MODEL - v.shape[0]))

    def mla_rows(g):
        rot = jnp.pad(_swap_halves(g[MLA_NOPE:], hr), (MLA_NOPE, 0))
        return row(g), row(rot)

    def diff_rows(g):
        n = DIFF_WIDTH // DIFF_QK
        return row(jnp.tile(g, n)), row(jnp.tile(_swap_halves(g, hd), n))

    rows = [row(norm_g), row(q_lat_g), row(kv_lat_g), *mla_rows(mla_q_g), *mla_rows(mla_k_g),
            row(jnp.tile(na_q_g, NA_HEADS)), row(jnp.tile(na_k_g, NA_HEADS)),
            *diff_rows(diff_q_g), *diff_rows(diff_k_g)]
    rows += [jnp.zeros((D_MODEL,), F32)] * (G_ROWS - len(rows))
    return w_ext, wuq, wuk, wuvt, wnvt, wdvt, jnp.stack(rows)


def _rope_tables(S):
    def cs(dim):
        inv = ROPE_THETA ** (-jnp.arange(0, dim, 2, dtype=F32) / dim)
        ang = jnp.arange(S, dtype=F32)[:, None] * inv[None, :]
        return jnp.cos(ang), jnp.sin(ang)

    c, s = cs(MLA_ROPE)
    pad = LANES - MLA_QK
    cm = jnp.concatenate([jnp.ones((S, MLA_NOPE), F32), c, c, jnp.zeros((S, pad), F32)], axis=-1)
    sm = jnp.concatenate([jnp.zeros((S, MLA_NOPE), F32), s, s, jnp.zeros((S, pad), F32)], axis=-1)
    c, s = cs(DIFF_QK)
    n = LANES // DIFF_QK
    cd = jnp.tile(jnp.concatenate([c, c], axis=-1), (1, n))
    sd = jnp.tile(jnp.concatenate([s, s], axis=-1), (1, n))
    return cm, sm, cd, sd


def _tile(S, pref):
    t = pref
    while S % t:
        t //= 2
    return t


def kernel(x, c, ada_w, ada_b, norm_g, w_in, q_lat_g, w_uq, kv_lat_g, w_ukv, mla_q_g, mla_k_g,
           na_q_g, na_k_g, na_rpb, diff_q_g, diff_k_g, lam_q1, lam_k1, lam_q2, lam_k2, subln_g, w_out):
    B, S, D = x.shape
    L = ada_w.shape[0]
    rows = S // GRID_W
    assert D == D_MODEL and S % (NA_RB * GRID_W) == 0 and rows >= NA_KROWS

    tabs = _rope_tables(S)
    mod = _adaln(c, ada_w, ada_b)
    tm = _tile(S, TILE_M)
    tk = _tile(tm, TILE_K)

    h = x
    for i in range(L):
        lam_init = 0.8 - 0.6 * math.exp(-0.3 * i)
        w_ext, wuq, wuk, wuvt, wnvt, wdvt, gv = _prep_layer(
            w_in[i], w_uq[i], w_ukv[i], norm_g[i], q_lat_g[i], kv_lat_g[i], mla_q_g[i], mla_k_g[i],
            na_q_g[i], na_k_g[i], diff_q_g[i], diff_k_g[i])
        mod_b = mod[i].transpose(1, 0, 2)
        qm, km, vmt, nq, nk, nvt, dq, dk, dvt, gt = _inproj(h, mod_b, gv, w_ext, wuq, wuk, wuvt, wnvt, wdvt,
                                                            tabs, tm, tk)
        om = _mla_attn(qm, km, vmt, _tile(S, TILE_Q_MLA))
        on = _na_attn(nq, nk, nvt, _na_bias_table(na_rpb[i], rows))
        lamv = jnp.stack([lam_q1[i], lam_k1[i], lam_q2[i], lam_k2[i]]).astype(F32)
        tqd = _tile(S, TILE_Q_DIFF)
        sgt = jnp.broadcast_to(subln_g[i].astype(F32)[:, None], (DIFF_V, tqd))
        od = _diff_attn(dq, dk, dvt, lamv, sgt, lam_init, tqd)
        h = _outproj(h, om, on, od, gt, mod_b, w_out[i].astype(BF16), tm)
    return h
```

```python
import functools
import math

import numpy as np
import jax
import jax.numpy as jnp
from jax import lax
from jax.experimental import pallas as pl
from jax.experimental.pallas import tpu as pltpu

F32 = jnp.float32
BF16 = jnp.bfloat16

D_MODEL = 1024
GRID_W = 64
EPS = 1e-6
ROPE_THETA = 10000.0
MLA_HEADS, MLA_NOPE, MLA_ROPE, MLA_V = 6, 64, 32, 64
MLA_QK = MLA_NOPE + MLA_ROPE
Q_LORA, KV_LORA = 256, 128
NA_HEADS, NA_DIM, NA_KR, NA_KC = 6, 64, 8, 16
DIFF_HEADS, DIFF_QK = 4, 32
DIFF_V = 2 * DIFF_QK
MLA_WIDTH, NA_WIDTH, DIFF_WIDTH = MLA_HEADS * MLA_V, NA_HEADS * NA_DIM, DIFF_HEADS * DIFF_V
HEAD_V = 64

LANES = 128
BF16_SUBLANES = 16
MXU_N = 256
VMEM_LIMIT = 52 * 1024 * 1024

C_CQ, C_CKV, C_KPE, C_KPER = 0, 256, 384, 512
C_NA = 640
C_DIFF = 1408
C_GATE = 2432
N_EXT = 3456

G_NORM, G_QLAT, G_KVLAT, G_MQ, G_MQR, G_MK, G_MKR, G_NQ, G_NK, G_DQ, G_DQR, G_DK, G_DKR = range(13)
G_ROWS = 16

TILE_M = 512
TILE_Q_MLA = 1024
TILE_Q_DIFF = 1024
TILE_K = 512
QK_AHEAD = 3

NA_RB = 4
NA_KROWS = 12
NEG = -1e30
LOG2E = math.log2(math.e)
LAZY_MAX_RISE = 64.0


def _nt(a, b):
    return lax.dot_general(a, b, (((1,), (1,)), ((), ())), preferred_element_type=F32)


def _mm(a, b):
    return jnp.dot(a, b, preferred_element_type=F32)


def _seg_ones(seg, width):
    sh = int(math.log2(seg))
    r = lax.broadcasted_iota(jnp.int32, (width, width), 0) >> sh
    c = lax.broadcasted_iota(jnp.int32, (width, width), 1) >> sh
    return jnp.where(r == c, 1.0, 0.0).astype(BF16)


def _seg_rms_scale(x, seg):
    x2 = (x * x).astype(BF16)
    width = x.shape[1]
    parts = []
    for a in range(0, width, MXU_N):
        w = min(MXU_N, width - a)
        parts.append(_mm(x2[:, a:a + w], _seg_ones(seg, w)))
    ss = parts[0] if len(parts) == 1 else jnp.concatenate(parts, axis=1)
    return lax.rsqrt(ss * (1.0 / seg) + EPS)


def _resident(a):
    zeros = (0,) * a.ndim
    return pl.BlockSpec(a.shape, lambda *_: zeros, pipeline_mode=pl.Buffered(1))


def _adaln_body(c_ref, w_ref, b_ref, o_ref):
    c = c_ref[...]
    a = c / (1.0 + jnp.exp(-c))
    o_ref[0, 0] = jnp.dot(a, w_ref[0], preferred_element_type=F32,
                          precision=lax.Precision.HIGHEST) + b_ref[0, 0]


def _adaln(c, ada_w, ada_b):
    L = ada_w.shape[0]
    B = c.shape[0]
    b4 = ada_b.reshape(L, 3, 1, D_MODEL)
    return pl.pallas_call(
        _adaln_body,
        out_shape=jax.ShapeDtypeStruct((L, 3, B, D_MODEL), F32),
        grid=(L, 3),
        in_specs=[
            pl.BlockSpec((B, D_MODEL), lambda l, j: (0, 0)),
            pl.BlockSpec((1, D_MODEL, D_MODEL), lambda l, j: (l, 0, j)),
            pl.BlockSpec((1, 1, 1, D_MODEL), lambda l, j: (l, j, 0, 0)),
        ],
        out_specs=pl.BlockSpec((1, 1, B, D_MODEL), lambda l, j: (l, j, 0, 0)),
        compiler_params=pltpu.CompilerParams(
            dimension_semantics=("arbitrary", "arbitrary"), vmem_limit_bytes=VMEM_LIMIT),
        name="adaln",
    )(c, ada_w, b4)


def _inproj_body(x_ref, mod_ref, gv_ref, w_ref, wuq_ref, wuk_ref, wuvt_ref, wnvt_ref, wdvt_ref,
                 cm_ref, sm_ref, cd_ref, sd_ref,
                 qm_ref, km_ref, vmt_ref, nq_ref, nk_ref, nvt_ref, dq_ref, dk_ref, dvt_ref, gt_ref):
    def gain(row, width, off=0):
        return gv_ref[row:row + 1, off:off + width]

    x = x_ref[0]
    ms = jnp.mean(x * x, axis=-1, keepdims=True)
    xn = x * lax.rsqrt(ms + EPS) * gain(G_NORM, D_MODEL)
    h = (xn * (1.0 + mod_ref[0, 1:2, :]) + mod_ref[0, 0:1, :]).astype(BF16)

    def proj(a, b):
        return _mm(h, w_ref[:, a:b])

    def lat_norm(v, row, width):
        r = lax.rsqrt(jnp.mean(v * v, axis=-1, keepdims=True) + EPS)
        return (v * r * gain(row, width)).astype(BF16)

    cm, sm = cm_ref[...], sm_ref[...]
    cd, sd = cd_ref[...], sd_ref[...]

    cqn = lat_norm(proj(C_CQ, C_CQ + Q_LORA), G_QLAT, Q_LORA)
    qq = _mm(cqn, wuq_ref[...])
    wq_c = gain(G_MQ, LANES) * cm
    wq_s = gain(G_MQR, LANES) * sm
    q_scale = MLA_QK ** -0.5 * LOG2E
    for hd in range(MLA_HEADS):
        a = qq[:, hd * LANES:(hd + 1) * LANES]
        b = qq[:, (MLA_HEADS + hd) * LANES:(MLA_HEADS + hd + 1) * LANES]
        r = lax.rsqrt(jnp.sum(a * a, axis=-1, keepdims=True) * (1.0 / MLA_QK) + EPS)
        qm_ref[0, :, hd * LANES:(hd + 1) * LANES] = ((r * q_scale) * (a * wq_c + b * wq_s)).astype(BF16)

    ckvn = lat_norm(proj(C_CKV, C_CKV + KV_LORA), G_KVLAT, KV_LORA)
    _store_chunks(vmt_ref, _nt(wuvt_ref[...], ckvn).astype(BF16))
    kn = _mm(ckvn, wuk_ref[...])
    kpe2 = proj(C_KPE, C_KPE + 2 * LANES)
    kpe, kper = kpe2[:, :LANES], kpe2[:, LANES:]
    wk_c = gain(G_MK, LANES) * cm
    k_rot = kper * (gain(G_MKR, LANES) * sm)
    for hd in range(MLA_HEADS):
        a = kn[:, hd * LANES:(hd + 1) * LANES] + kpe
        r = lax.rsqrt(jnp.sum(a * a, axis=-1, keepdims=True) * (1.0 / MLA_QK) + EPS)
        km_ref[0, :, hd * LANES:(hd + 1) * LANES] = (r * (a * wk_c + k_rot)).astype(BF16)

    nqk = proj(C_NA, C_NA + 2 * NA_WIDTH)
    na_scale = NA_DIM ** -0.5 * LOG2E
    a = nqk[:, :NA_WIDTH]
    nq_ref[0] = (a * (_seg_rms_scale(a, NA_DIM) * na_scale) * gain(G_NQ, NA_WIDTH)).astype(BF16)
    a = nqk[:, NA_WIDTH:]
    nk_ref[0] = (a * _seg_rms_scale(a, NA_DIM) * gain(G_NK, NA_WIDTH)).astype(BF16)
    _store_chunks(nvt_ref, _nt(wnvt_ref[...], h).astype(BF16))

    dd = proj(C_DIFF, C_DIFF + 4 * DIFF_WIDTH)
    d_scale = DIFF_QK ** -0.5 * LOG2E
    reps = DIFF_WIDTH // LANES
    cdw, sdw = jnp.concatenate([cd] * reps, axis=1), jnp.concatenate([sd] * reps, axis=1)
    for (o_ref, base, rot, grow, grot, sc) in (
            (dq_ref, 0, 2 * DIFF_WIDTH, G_DQ, G_DQR, d_scale),
            (dk_ref, DIFF_WIDTH, 3 * DIFF_WIDTH, G_DK, G_DKR, 1.0)):
        a = dd[:, base:base + DIFF_WIDTH]
        b = dd[:, rot:rot + DIFF_WIDTH]
        r = _seg_rms_scale(a, DIFF_QK)
        o_ref[0] = ((r * sc) * (a * (gain(grow, DIFF_WIDTH) * cdw) + b * (gain(grot, DIFF_WIDTH) * sdw))).astype(BF16)
    _store_chunks(dvt_ref, _nt(wdvt_ref[...], h).astype(BF16))

    g = proj(C_GATE, N_EXT)
    gt_ref[0] = (g / (1.0 + jnp.exp(-g))).astype(BF16)


def _store_chunks(ref, val):
    tk = ref.shape[3]
    for c in range(ref.shape[1]):
        ref[0, c] = val[:, c * tk:(c + 1) * tk]


def _inproj(x, mod_b, gv, w_ext, wuq, wuk, wuvt, wnvt, wdvt, tabs, tm, tk):
    B, S, _ = x.shape
    nt = S // tm
    cm, sm, cd, sd = tabs
    tok = lambda w: pl.BlockSpec((1, tm, w), lambda b, i: (b, i, 0))
    tr = lambda w, c: pl.BlockSpec((1, tm // c, w, c), lambda b, i: (b, i, 0, 0))
    tab = pl.BlockSpec((tm, LANES), lambda b, i: (i, 0))
    tok_out = lambda w: jax.ShapeDtypeStruct((B, S, w), BF16)
    tr_out = lambda w, c: jax.ShapeDtypeStruct((B, S // c, w, c), BF16)
    tna = NA_RB * GRID_W
    weights = (gv, w_ext, wuq, wuk, wuvt, wnvt, wdvt)
    return pl.pallas_call(
        _inproj_body,
        out_shape=[tok_out(MLA_HEADS * LANES), tok_out(MLA_HEADS * LANES), tr_out(MLA_WIDTH, tk),
                   tok_out(NA_WIDTH), tok_out(NA_WIDTH), tr_out(NA_WIDTH, tna),
                   tok_out(DIFF_WIDTH), tok_out(DIFF_WIDTH), tr_out(DIFF_WIDTH, tk), tok_out(D_MODEL)],
        grid=(B, nt),
        in_specs=[tok(D_MODEL), pl.BlockSpec((1, 3, D_MODEL), lambda b, i: (b, 0, 0)),
                  *[_resident(a) for a in weights], tab, tab, tab, tab],
        out_specs=[tok(MLA_HEADS * LANES), tok(MLA_HEADS * LANES), tr(MLA_WIDTH, tk),
                   tok(NA_WIDTH), tok(NA_WIDTH), tr(NA_WIDTH, tna),
                   tok(DIFF_WIDTH), tok(DIFF_WIDTH), tr(DIFF_WIDTH, tk), tok(D_MODEL)],
        compiler_params=pltpu.CompilerParams(
            dimension_semantics=("parallel", "parallel"), vmem_limit_bytes=VMEM_LIMIT),
        name="inproj",
    )(x, mod_b, *weights, cm, sm, cd, sd)


def _flash_chains(k_ref, vt_ref, qs, kblk, vhead, nk, m_scr, acc_scr):
    tk = vt_ref.shape[3]
    n = len(qs)

    def operands(j):
        off = pl.multiple_of(j * tk, tk)
        k = k_ref[0, pl.ds(off, tk), :]
        vt = vt_ref[0, j]
        ones = jnp.ones((BF16_SUBLANES, tk), BF16)
        vts = {h: jnp.concatenate([vt[h * HEAD_V:(h + 1) * HEAD_V], ones], axis=0) for h in set(vhead)}
        return k, vts

    def scores(k, g):
        return _nt(k[:, kblk[g] * LANES:(kblk[g] + 1) * LANES], qs[g])

    def issue_ahead(k):
        pending = [scores(k, g) for g in range(min(QK_AHEAD, n))]
        for g in range(n):
            s = pending.pop(0)
            if g + QK_AHEAD < n:
                pending.append(scores(k, g + QK_AHEAD))
            yield g, s

    def exact_tile(j):
        k, vts = operands(j)
        for g, s in issue_ahead(k):
            m = m_scr[g]
            mn = jnp.maximum(m, jnp.max(s, axis=0, keepdims=True))
            p = jnp.exp2(s - mn).astype(BF16)
            acc_scr[g] = acc_scr[g] * jnp.exp2(m - mn) + _mm(vts[vhead[g]], p)
            m_scr[g] = mn

    def lazy_tile(j):
        k, vts = operands(j)
        new, rise = [], None
        for g, s in issue_ahead(k):
            m = m_scr[g]
            top = jnp.max(s, axis=0, keepdims=True)
            p = jnp.exp2(s - m).astype(BF16)
            mn = jnp.maximum(m, top)
            new.append((mn, (acc_scr[g] + _mm(vts[vhead[g]], p)) * jnp.exp2(m - mn)))
            rise = top - m if rise is None else jnp.maximum(rise, top - m)
        safe = jnp.max(rise) <= LAZY_MAX_RISE

        @pl.when(safe)
        def _():
            for g, (mn, acc) in enumerate(new):
                m_scr[g] = mn
                acc_scr[g] = acc

        @pl.when(jnp.logical_not(safe))
        def _():
            exact_tile(j)

    k0 = k_ref[0, 0:BF16_SUBLANES, :]
    for g in range(n):
        m_scr[g] = jnp.max(scores(k0, g), axis=0, keepdims=True)
        acc_scr[g] = jnp.zeros(acc_scr.shape[1:], F32)

    def step(j, c):
        lazy_tile(j)
        return c

    lax.fori_loop(0, nk, step, 0)
    return [acc_scr[g] for g in range(n)]


def _normalised(acc):
    return acc[:HEAD_V] / acc[HEAD_V:HEAD_V + 1]


def _mla_body(q_ref, k_ref, vt_ref, o_ref, m_scr, acc_scr, *, nk):
    heads = list(range(MLA_HEADS))
    qs = [q_ref[0, :, h * LANES:(h + 1) * LANES] for h in heads]
    accs = _flash_chains(k_ref, vt_ref, qs, heads, heads, nk, m_scr, acc_scr)
    o_ref[0] = jnp.concatenate([_normalised(a) for a in accs], axis=0).T.astype(o_ref.dtype)


def _dense_attn_call(body, name, q, k, vt, extra, out_width, tq, chains):
    B, S, wq = q.shape
    nk, wv, tk = vt.shape[1:]
    return pl.pallas_call(
        body,
        out_shape=jax.ShapeDtypeStruct((B, S, out_width), BF16),
        grid=(B, S // tq),
        in_specs=[pl.BlockSpec((1, tq, wq), lambda b, i: (b, i, 0)),
                  pl.BlockSpec((1, S, wq), lambda b, i: (b, 0, 0)),
                  pl.BlockSpec((1, nk, wv, tk), lambda b, i: (b, 0, 0, 0)),
                  *[pl.BlockSpec(a.shape, lambda b, i: (0, 0)) for a in extra]],
        out_specs=pl.BlockSpec((1, tq, out_width), lambda b, i: (b, i, 0)),
        scratch_shapes=[pltpu.VMEM((chains, 1, tq), F32),
                        pltpu.VMEM((chains, HEAD_V + BF16_SUBLANES, tq), F32)],
        compiler_params=pltpu.CompilerParams(
            dimension_semantics=("parallel", "arbitrary"), vmem_limit_bytes=VMEM_LIMIT),
        name=name,
    )(q, k, vt, *extra)


def _mla_attn(q, k, vt, tq):
    return _dense_attn_call(functools.partial(_mla_body, nk=vt.shape[1]), "mla_attn", q, k, vt, (), MLA_WIDTH, tq,
                            MLA_HEADS)


def _diff_body(q_ref, k_ref, vt_ref, lam_ref, sg_ref, o_ref, m_scr, acc_scr, *, nk, lam_init):
    lane = lax.broadcasted_iota(jnp.int32, (1, LANES), 1)
    per_blk = LANES // DIFF_QK
    chains = list(range(2 * DIFF_HEADS))
    qs = []
    for g in chains:
        qb = q_ref[0, :, (g // per_blk) * LANES:(g // per_blk + 1) * LANES]
        lo = (g % per_blk) * DIFF_QK
        qs.append(jnp.where((lane >= lo) & (lane < lo + DIFF_QK), qb, jnp.zeros_like(qb)))
    accs = _flash_chains(k_ref, vt_ref, qs, [g // per_blk for g in chains], [g // 2 for g in chains], nk,
                         m_scr, acc_scr)

    lv = lam_ref[...]
    lam = (jnp.exp(jnp.sum(lv[0:1] * lv[1:2], axis=-1, keepdims=True))
           - jnp.exp(jnp.sum(lv[2:3] * lv[3:4], axis=-1, keepdims=True)) + lam_init)
    outs = []
    for h in range(DIFF_HEADS):
        o = _normalised(accs[2 * h]) - lam * _normalised(accs[2 * h + 1])
        r = lax.rsqrt(jnp.sum(o * o, axis=0, keepdims=True) * (1.0 / DIFF_V) + EPS)
        outs.append(o * r * (sg_ref[...] * (1.0 - lam_init)))
    o_ref[0] = jnp.concatenate(outs, axis=0).T.astype(o_ref.dtype)


def _diff_attn(q, k, vt, lamv, sgt, lam_init, tq):
    body = functools.partial(_diff_body, nk=vt.shape[1], lam_init=lam_init)
    return _dense_attn_call(body, "diff_attn", q, k, vt, (lamv, sgt), DIFF_WIDTH, tq, 2 * DIFF_HEADS)


def _na_body(q_ref, k_ref, vt_ref, bm_ref, o_ref, *, nrb):
    rb = pl.program_id(1)
    variant = jnp.where(rb == 0, 0, jnp.where(rb == nrb - 1, 2, 1))
    chunk0 = jnp.clip(rb - 1, 0, nrb - 3)
    tq = NA_RB * GRID_W
    off = pl.multiple_of(chunk0 * tq, tq)
    nkeys = NA_KROWS * GRID_W
    k = k_ref[0, pl.ds(off, nkeys), :]
    vt = jnp.concatenate([vt_ref[0, chunk0 + c] for c in range(NA_KROWS // NA_RB)], axis=1)
    ones = jnp.ones((BF16_SUBLANES, nkeys), BF16)
    lane = lax.broadcasted_iota(jnp.int32, (1, LANES), 1)
    per_blk = LANES // NA_DIM

    def scores(h):
        blk = slice((h // per_blk) * LANES, (h // per_blk + 1) * LANES)
        qb = q_ref[0, :, blk]
        lo = (h % per_blk) * NA_DIM
        qh = jnp.where((lane >= lo) & (lane < lo + NA_DIM), qb, jnp.zeros_like(qb))
        return _nt(k[:, blk], qh) + bm_ref[h, variant]

    pending = [scores(h) for h in range(min(QK_AHEAD, NA_HEADS))]
    outs = []
    for h in range(NA_HEADS):
        s = pending.pop(0)
        if h + QK_AHEAD < NA_HEADS:
            pending.append(scores(h + QK_AHEAD))
        p = jnp.exp2(s - jnp.max(s, axis=0, keepdims=True)).astype(BF16)
        vt_ext = jnp.concatenate([vt[h * NA_DIM:(h + 1) * NA_DIM], ones], axis=0)
        acc = _mm(vt_ext, p)
        outs.append(acc[:NA_DIM] / acc[NA_DIM:NA_DIM + 1])
    o_ref[0] = jnp.concatenate(outs, axis=0).T.astype(o_ref.dtype)


def _na_attn(q, k, vt, bm):
    B, S, _ = q.shape
    tq = NA_RB * GRID_W
    nrb = S // tq
    return pl.pallas_call(
        functools.partial(_na_body, nrb=nrb),
        out_shape=jax.ShapeDtypeStruct((B, S, NA_WIDTH), BF16),
        grid=(B, nrb),
        in_specs=[pl.BlockSpec((1, tq, NA_WIDTH), lambda b, i: (b, i, 0)),
                  pl.BlockSpec((1, S, NA_WIDTH), lambda b, i: (b, 0, 0)),
                  pl.BlockSpec((1, nrb, NA_WIDTH, tq), lambda b, i: (b, 0, 0, 0)),
                  _resident(bm)],
        out_specs=pl.BlockSpec((1, tq, NA_WIDTH), lambda b, i: (b, i, 0)),
        compiler_params=pltpu.CompilerParams(
            dimension_semantics=("parallel", "arbitrary"), vmem_limit_bytes=VMEM_LIMIT),
        name="na_attn",
    )(q, k, vt, bm)


def _na_bias_table(rpb, rows):
    nrb = rows // NA_RB
    W = GRID_W
    H = rpb.shape[0]
    pad_c = W - NA_KC
    rp = jnp.pad(rpb.astype(F32), ((0, 0), (0, 0), (pad_c, pad_c + 1)))
    skew = jnp.tile(rp, (1, 1, W))[:, :, :W * (2 * W - 1)].reshape(H, -1, W, 2 * W - 1)
    t = skew[:, :, :, W - 1:].transpose(0, 2, 1, 3)
    pad_lo, pad_hi = NA_KR // 2, NA_KROWS - NA_KR
    t = jnp.pad(t, ((0, 0), (0, 0), (pad_lo, pad_hi), (0, 0)))
    c = np.arange(W)[:, None, None]
    kri = np.arange(NA_KROWS)[None, :, None]
    kc = np.arange(W)[None, None, :]
    c0 = np.clip(c - NA_KC // 2, 0, W - NA_KC)
    col_ok = (kc >= c0) & (kc < c0 + NA_KC)
    blocks = []
    for rb in (0, 1, nrb - 1):
        start = NA_RB * min(max(rb - 1, 0), nrb - 3)
        for ri in range(NA_RB):
            r = NA_RB * rb + ri
            r0 = min(max(r - NA_KR // 2, 0), rows - NA_KR)
            d0 = start - r + (NA_KR - 1) + pad_lo
            assert 0 <= d0 and d0 + NA_KROWS <= t.shape[2]
            ok = ((start + kri >= r0) & (start + kri < r0 + NA_KR)) & col_ok
            blk = jnp.where(ok[None], t[:, :, d0:d0 + NA_KROWS, :] * LOG2E, NEG)
            blocks.append(blk.reshape(H, W, NA_KROWS * W))
    table = jnp.stack(blocks, axis=1).reshape(H, 3, NA_RB * W, NA_KROWS * W)
    return table.transpose(0, 1, 3, 2)


def _outproj_body(x_ref, om_ref, on_ref, od_ref, gt_ref, gate_ref, w_ref, o_ref):
    def branch(o_ref_, a, b):
        mix = (o_ref_[0].astype(F32) * gt_ref[0, :, a:b].astype(F32)).astype(BF16)
        return _mm(mix, w_ref[a:b, :])

    y = (branch(om_ref, 0, MLA_WIDTH) + branch(on_ref, MLA_WIDTH, MLA_WIDTH + NA_WIDTH)
         + branch(od_ref, MLA_WIDTH + NA_WIDTH, D_MODEL))
    o_ref[0] = x_ref[0] + gate_ref[0, 2:3, :] * y


def _outproj(x, om, on, od, gt, mod_b, w_out, tm):
    B, S, _ = x.shape
    tok = lambda w: pl.BlockSpec((1, tm, w), lambda b, i: (b, i, 0))
    return pl.pallas_call(
        _outproj_body,
        out_shape=jax.ShapeDtypeStruct((B, S, D_MODEL), F32),
        grid=(B, S // tm),
        in_specs=[tok(D_MODEL), tok(MLA_WIDTH), tok(NA_WIDTH), tok(DIFF_WIDTH), tok(D_MODEL),
                  pl.BlockSpec((1, 3, D_MODEL), lambda b, i: (b, 0, 0)),
                  _resident(w_out)],
        out_specs=tok(D_MODEL),
        compiler_params=pltpu.CompilerParams(
            dimension_semantics=("parallel", "parallel"), vmem_limit_bytes=VMEM_LIMIT),
        name="outproj",
    )(x, om, on, od, gt, mod_b, w_out)


def _rot_cols(w, half):
    return jnp.concatenate([-w[..., half:], w[..., :half]], axis=-1)


def _swap_halves(g, half):
    return jnp.concatenate([g[..., half:], g[..., :half]], axis=-1)


def _pad_cols(w, left, total):
    return jnp.pad(w, ((0, 0), (left, total - left - w.shape[-1])))


def _prep_layer(w_in, w_uq, w_ukv, norm_g, q_lat_g, kv_lat_g, mla_q_g, mla_k_g,
                na_q_g, na_k_g, diff_q_g, diff_k_g):
    sizes = (Q_LORA, KV_LORA, MLA_ROPE, 3 * NA_WIDTH, 3 * DIFF_WIDTH, D_MODEL)
    splits = np.cumsum(sizes)[:-1].tolist()
    w_cq, w_ckv, w_kpe, w_na, w_diff, w_gate = jnp.split(w_in, splits, axis=-1)
    hr = MLA_ROPE // 2
    hd = DIFF_QK // 2
    w_qd, w_kd, w_vd = jnp.split(w_diff, 3, axis=-1)
    w_nq, w_nk, w_nv = jnp.split(w_na, 3, axis=-1)
    rot_d = lambda w: _rot_cols(w.reshape(D_MODEL, -1, DIFF_QK), hd).reshape(D_MODEL, DIFF_WIDTH)
    w_ext = jnp.concatenate([
        w_cq, w_ckv,
        _pad_cols(w_kpe, MLA_NOPE, LANES), _pad_cols(_rot_cols(w_kpe, hr), MLA_NOPE, LANES),
        w_nq, w_nk, w_qd, w_kd, rot_d(w_qd), rot_d(w_kd), w_gate], axis=-1).astype(BF16)
    wnvt = w_nv.T.astype(BF16)
    wdvt = w_vd.T.astype(BF16)

    uq = w_uq.reshape(Q_LORA, MLA_HEADS, MLA_QK)
    uq_raw = jnp.pad(uq, ((0, 0), (0, 0), (0, LANES - MLA_QK)))
    uq_rot = jnp.pad(_rot_cols(uq[..., MLA_NOPE:], hr), ((0, 0), (0, 0), (MLA_NOPE, LANES - MLA_QK)))
    wuq = jnp.concatenate([uq_raw.reshape(Q_LORA, -1), uq_rot.reshape(Q_LORA, -1)], axis=-1).astype(BF16)

    ukv = w_ukv.reshape(KV_LORA, MLA_HEADS, MLA_NOPE + MLA_V)
    wuk = jnp.pad(ukv[..., :MLA_NOPE], ((0, 0), (0, 0), (0, LANES - MLA_NOPE))).reshape(KV_LORA, -1).astype(BF16)
    wuvt = ukv[..., MLA_NOPE:].reshape(KV_LORA, -1).T.astype(BF16)

    def row(v):
        return jnp.pad(v.astype(F32), (0, D_MODEL - v.shape[0]))

    def mla_rows(g):
        rot = jnp.pad(_swap_halves(g[MLA_NOPE:], hr), (MLA_NOPE, 0))
        return row(g), row(rot)

    def diff_rows(g):
        n = DIFF_WIDTH // DIFF_QK
        return row(jnp.tile(g, n)), row(jnp.tile(_swap_halves(g, hd), n))

    rows = [row(norm_g), row(q_lat_g), row(kv_lat_g), *mla_rows(mla_q_g), *mla_rows(mla_k_g),
            row(jnp.tile(na_q_g, NA_HEADS)), row(jnp.tile(na_k_g, NA_HEADS)),
            *diff_rows(diff_q_g), *diff_rows(diff_k_g)]
    rows += [jnp.zeros((D_MODEL,), F32)] * (G_ROWS - len(rows))
    return w_ext, wuq, wuk, wuvt, wnvt, wdvt, jnp.stack(rows)


def _rope_tables(S):
    def cs(dim):
        inv = ROPE_THETA ** (-jnp.arange(0, dim, 2, dtype=F32) / dim)
        ang = jnp.arange(S, dtype=F32)[:, None] * inv[None, :]
        return jnp.cos(ang), jnp.sin(ang)

    c, s = cs(MLA_ROPE)
    pad = LANES - MLA_QK
    cm = jnp.concatenate([jnp.ones((S, MLA_NOPE), F32), c, c, jnp.zeros((S, pad), F32)], axis=-1)
    sm = jnp.concatenate([jnp.zeros((S, MLA_NOPE), F32), s, s, jnp.zeros((S, pad), F32)], axis=-1)
    c, s = cs(DIFF_QK)
    n = LANES // DIFF_QK
    cd = jnp.tile(jnp.concatenate([c, c], axis=-1), (1, n))
    sd = jnp.tile(jnp.concatenate([s, s], axis=-1), (1, n))
    return cm, sm, cd, sd


def _tile(S, pref):
    t = pref
    while S % t:
        t //= 2
    return t


def kernel(x, c, ada_w, ada_b, norm_g, w_in, q_lat_g, w_uq, kv_lat_g, w_ukv, mla_q_g, mla_k_g,
           na_q_g, na_k_g, na_rpb, diff_q_g, diff_k_g, lam_q1, lam_k1, lam_q2, lam_k2, subln_g, w_out):
    B, S, D = x.shape
    L = ada_w.shape[0]
    rows = S // GRID_W
    assert D == D_MODEL and S % (NA_RB * GRID_W) == 0 and rows >= NA_KROWS

    tabs = _rope_tables(S)
    mod = _adaln(c, ada_w, ada_b)
    tm = _tile(S, TILE_M)
    tk = _tile(tm, TILE_K)

    h = x
    for i in range(L):
        lam_init = 0.8 - 0.6 * math.exp(-0.3 * i)
        w_ext, wuq, wuk, wuvt, wnvt, wdvt, gv = _prep_layer(
            w_in[i], w_uq[i], w_ukv[i], norm_g[i], q_lat_g[i], kv_lat_g[i], mla_q_g[i], mla_k_g[i],
            na_q_g[i], na_k_g[i], diff_q_g[i], diff_k_g[i])
        mod_b = mod[i].transpose(1, 0, 2)
        qm, km, vmt, nq, nk, nvt, dq, dk, dvt, gt = _inproj(h, mod_b, gv, w_ext, wuq, wuk, wuvt, wnvt, wdvt,
                                                            tabs, tm, tk)
        om = _mla_attn(qm, km, vmt, _tile(S, TILE_Q_MLA))
        on = _na_attn(nq, nk, nvt, _na_bias_table(na_rpb[i], rows))
        lamv = jnp.stack([lam_q1[i], lam_k1[i], lam_q2[i], lam_k2[i]]).astype(F32)
        tqd = _tile(S, TILE_Q_DIFF)
        sgt = jnp.broadcast_to(subln_g[i].astype(F32)[:, None], (DIFF_V, tqd))
        od = _diff_attn(dq, dk, dvt, lamv, sgt, lam_init, tqd)
        h = _outproj(h, om, on, od, gt, mod_b, w_out[i].astype(BF16), tm)
    return h
```

```python
import functools
import math

import numpy as np
import jax
import jax.numpy as jnp
from jax import lax
from jax.experimental import pallas as pl
from jax.experimental.pallas import tpu as pltpu

F32 = jnp.float32
BF16 = jnp.bfloat16

D_MODEL = 1024
GRID_W = 64
EPS = 1e-6
ROPE_THETA = 10000.0
MLA_HEADS, MLA_NOPE, MLA_ROPE, MLA_V = 6, 64, 32, 64
MLA_QK = MLA_NOPE + MLA_ROPE
Q_LORA, KV_LORA = 256, 128
NA_HEADS, NA_DIM, NA_KR, NA_KC = 6, 64, 8, 16
DIFF_HEADS, DIFF_QK = 4, 32
DIFF_V = 2 * DIFF_QK
MLA_WIDTH, NA_WIDTH, DIFF_WIDTH = MLA_HEADS * MLA_V, NA_HEADS * NA_DIM, DIFF_HEADS * DIFF_V
HEAD_V = 64

LANES = 128
BF16_SUBLANES = 16
MXU_N = 256
VMEM_LIMIT = 52 * 1024 * 1024

C_CQ, C_CKV, C_KPE, C_KPER = 0, 256, 384, 512
C_NA = 640
C_DIFF = 1408
C_GATE = 2432
N_EXT = 3456

G_NORM, G_QLAT, G_KVLAT, G_MQ, G_MQR, G_MK, G_MKR, G_NQ, G_NK, G_DQ, G_DQR, G_DK, G_DKR = range(13)
G_ROWS = 16

TILE_M = 1024
TILE_Q_MLA = 1024
TILE_Q_DIFF = 1024
TILE_K = 512
QK_AHEAD = 3

NA_RB = 4
NA_KROWS = 12
NEG = -1e30
LOG2E = math.log2(math.e)
LAZY_MAX_RISE = 64.0


def _nt(a, b):
    return lax.dot_general(a, b, (((1,), (1,)), ((), ())), preferred_element_type=F32)


def _mm(a, b):
    return jnp.dot(a, b, preferred_element_type=F32)


def _seg_ones(seg, width):
    sh = int(math.log2(seg))
    r = lax.broadcasted_iota(jnp.int32, (width, width), 0) >> sh
    c = lax.broadcasted_iota(jnp.int32, (width, width), 1) >> sh
    return jnp.where(r == c, 1.0, 0.0).astype(BF16)


def _seg_rms_scale(x, seg):
    x2 = (x * x).astype(BF16)
    width = x.shape[1]
    parts = []
    for a in range(0, width, MXU_N):
        w = min(MXU_N, width - a)
        parts.append(_mm(x2[:, a:a + w], _seg_ones(seg, w)))
    ss = parts[0] if len(parts) == 1 else jnp.concatenate(parts, axis=1)
    return lax.rsqrt(ss * (1.0 / seg) + EPS)


def _resident(a):
    zeros = (0,) * a.ndim
    return pl.BlockSpec(a.shape, lambda *_: zeros, pipeline_mode=pl.Buffered(1))


def _adaln_body(c_ref, w_ref, b_ref, o_ref):
    c = c_ref[...]
    a = c / (1.0 + jnp.exp(-c))
    o_ref[0, 0] = jnp.dot(a, w_ref[0], preferred_element_type=F32,
                          precision=lax.Precision.HIGHEST) + b_ref[0, 0]


def _adaln(c, ada_w, ada_b):
    L = ada_w.shape[0]
    B = c.shape[0]
    b4 = ada_b.reshape(L, 3, 1, D_MODEL)
    return pl.pallas_call(
        _adaln_body,
        out_shape=jax.ShapeDtypeStruct((L, 3, B, D_MODEL), F32),
        grid=(L, 3),
        in_specs=[
            pl.BlockSpec((B, D_MODEL), lambda l, j: (0, 0)),
            pl.BlockSpec((1, D_MODEL, D_MODEL), lambda l, j: (l, 0, j)),
            pl.BlockSpec((1, 1, 1, D_MODEL), lambda l, j: (l, j, 0, 0)),
        ],
        out_specs=pl.BlockSpec((1, 1, B, D_MODEL), lambda l, j: (l, j, 0, 0)),
        compiler_params=pltpu.CompilerParams(
            dimension_semantics=("arbitrary", "arbitrary"), vmem_limit_bytes=VMEM_LIMIT),
        name="adaln",
    )(c, ada_w, b4)


def _inproj_body(x_ref, mod_ref, gv_ref, w_ref, wuq_ref, wuk_ref, wuvt_ref, wnvt_ref, wdvt_ref,
                 cm_ref, sm_ref, cd_ref, sd_ref,
                 qm_ref, km_ref, vmt_ref, nq_ref, nk_ref, nvt_ref, dq_ref, dk_ref, dvt_ref, gt_ref):
    def gain(row, width, off=0):
        return gv_ref[row:row + 1, off:off + width]

    x = x_ref[0]
    ms = jnp.mean(x * x, axis=-1, keepdims=True)
    xn = x * lax.rsqrt(ms + EPS) * gain(G_NORM, D_MODEL)
    h = (xn * (1.0 + mod_ref[0, 1:2, :]) + mod_ref[0, 0:1, :]).astype(BF16)

    def proj(a, b):
        return _mm(h, w_ref[:, a:b])

    def lat_norm(v, row, width):
        r = lax.rsqrt(jnp.mean(v * v, axis=-1, keepdims=True) + EPS)
        return (v * r * gain(row, width)).astype(BF16)

    cm, sm = cm_ref[...], sm_ref[...]
    cd, sd = cd_ref[...], sd_ref[...]

    cqn = lat_norm(proj(C_CQ, C_CQ + Q_LORA), G_QLAT, Q_LORA)
    qq = _mm(cqn, wuq_ref[...])
    wq_c = gain(G_MQ, LANES) * cm
    wq_s = gain(G_MQR, LANES) * sm
    q_scale = MLA_QK ** -0.5 * LOG2E
    for hd in range(MLA_HEADS):
        a = qq[:, hd * LANES:(hd + 1) * LANES]
        b = qq[:, (MLA_HEADS + hd) * LANES:(MLA_HEADS + hd + 1) * LANES]
        r = lax.rsqrt(jnp.sum(a * a, axis=-1, keepdims=True) * (1.0 / MLA_QK) + EPS)
        qm_ref[0, :, hd * LANES:(hd + 1) * LANES] = ((r * q_scale) * (a * wq_c + b * wq_s)).astype(BF16)

    ckvn = lat_norm(proj(C_CKV, C_CKV + KV_LORA), G_KVLAT, KV_LORA)
    _store_chunks(vmt_ref, _nt(wuvt_ref[...], ckvn).astype(BF16))
    kn = _mm(ckvn, wuk_ref[...])
    kpe2 = proj(C_KPE, C_KPE + 2 * LANES)
    kpe, kper = kpe2[:, :LANES], kpe2[:, LANES:]
    wk_c = gain(G_MK, LANES) * cm
    k_rot = kper * (gain(G_MKR, LANES) * sm)
    for hd in range(MLA_HEADS):
        a = kn[:, hd * LANES:(hd + 1) * LANES] + kpe
        r = lax.rsqrt(jnp.sum(a * a, axis=-1, keepdims=True) * (1.0 / MLA_QK) + EPS)
        km_ref[0, :, hd * LANES:(hd + 1) * LANES] = (r * (a * wk_c + k_rot)).astype(BF16)

    nqk = proj(C_NA, C_NA + 2 * NA_WIDTH)
    na_scale = NA_DIM ** -0.5 * LOG2E
    a = nqk[:, :NA_WIDTH]
    nq_ref[0] = (a * (_seg_rms_scale(a, NA_DIM) * na_scale) * gain(G_NQ, NA_WIDTH)).astype(BF16)
    a = nqk[:, NA_WIDTH:]
    nk_ref[0] = (a * _seg_rms_scale(a, NA_DIM) * gain(G_NK, NA_WIDTH)).astype(BF16)
    _store_chunks(nvt_ref, _nt(wnvt_ref[...], h).astype(BF16))

    dd = proj(C_DIFF, C_DIFF + 4 * DIFF_WIDTH)
    d_scale = DIFF_QK ** -0.5 * LOG2E
    reps = DIFF_WIDTH // LANES
    cdw, sdw = jnp.concatenate([cd] * reps, axis=1), jnp.concatenate([sd] * reps, axis=1)
    for (o_ref, base, rot, grow, grot, sc) in (
            (dq_ref, 0, 2 * DIFF_WIDTH, G_DQ, G_DQR, d_scale),
            (dk_ref, DIFF_WIDTH, 3 * DIFF_WIDTH, G_DK, G_DKR, 1.0)):
        a = dd[:, base:base + DIFF_WIDTH]
        b = dd[:, rot:rot + DIFF_WIDTH]
        r = _seg_rms_scale(a, DIFF_QK)
        o_ref[0] = ((r * sc) * (a * (gain(grow, DIFF_WIDTH) * cdw) + b * (gain(grot, DIFF_WIDTH) * sdw))).astype(BF16)
    _store_chunks(dvt_ref, _nt(wdvt_ref[...], h).astype(BF16))

    g = proj(C_GATE, N_EXT)
    gt_ref[0] = (g / (1.0 + jnp.exp(-g))).astype(BF16)


def _store_chunks(ref, val):
    tk = ref.shape[3]
    for c in range(ref.shape[1]):
        ref[0, c] = val[:, c * tk:(c + 1) * tk]


def _inproj(x, mod_b, gv, w_ext, wuq, wuk, wuvt, wnvt, wdvt, tabs, tm, tk):
    B, S, _ = x.shape
    nt = S // tm
    cm, sm, cd, sd = tabs
    tok = lambda w: pl.BlockSpec((1, tm, w), lambda b, i: (b, i, 0))
    def tr(w, c):
        if c <= tm:
            return pl.BlockSpec((1, tm // c, w, c), lambda b, i: (b, i, 0, 0))
        per = c // tm
        return pl.BlockSpec((1, 1, w, tm), lambda b, i: (b, i // per, 0, i % per))
    tab = pl.BlockSpec((tm, LANES), lambda b, i: (i, 0))
    tok_out = lambda w: jax.ShapeDtypeStruct((B, S, w), BF16)
    tr_out = lambda w, c: jax.ShapeDtypeStruct((B, S // c, w, c), BF16)
    tna = NA_RB * GRID_W
    weights = (gv, w_ext, wuq, wuk, wuvt, wnvt, wdvt)
    return pl.pallas_call(
        _inproj_body,
        out_shape=[tok_out(MLA_HEADS * LANES), tok_out(MLA_HEADS * LANES), tr_out(MLA_WIDTH, tk),
                   tok_out(NA_WIDTH), tok_out(NA_WIDTH), tr_out(NA_WIDTH, tna),
                   tok_out(DIFF_WIDTH), tok_out(DIFF_WIDTH), tr_out(DIFF_WIDTH, tk), tok_out(D_MODEL)],
        grid=(B, nt),
        in_specs=[tok(D_MODEL), pl.BlockSpec((1, 3, D_MODEL), lambda b, i: (b, 0, 0)),
                  *[_resident(a) for a in weights], tab, tab, tab, tab],
        out_specs=[tok(MLA_HEADS * LANES), tok(MLA_HEADS * LANES), tr(MLA_WIDTH, tk),
                   tok(NA_WIDTH), tok(NA_WIDTH), tr(NA_WIDTH, tna),
                   tok(DIFF_WIDTH), tok(DIFF_WIDTH), tr(DIFF_WIDTH, tk), tok(D_MODEL)],
        compiler_params=pltpu.CompilerParams(
            dimension_semantics=("parallel", "parallel"), vmem_limit_bytes=VMEM_LIMIT),
        name="inproj",
    )(x, mod_b, *weights, cm, sm, cd, sd)


def _flash_chains(k_ref, vt_ref, qs, kblk, vhead, nk, m_scr, acc_scr):
    tk = vt_ref.shape[3]
    n = len(qs)

    def operands(j):
        off = j * tk if isinstance(j, int) else pl.multiple_of(j * tk, tk)
        k = k_ref[0, pl.ds(off, tk), :]
        vt = vt_ref[0, j]
        ones = jnp.ones((BF16_SUBLANES, tk), BF16)
        vts = {h: jnp.concatenate([vt[h * HEAD_V:(h + 1) * HEAD_V], ones], axis=0) for h in set(vhead)}
        return k, vts

    def scores(k, g):
        return _nt(k[:, kblk[g] * LANES:(kblk[g] + 1) * LANES], qs[g])

    def issue_ahead(k):
        pending = [scores(k, g) for g in range(min(QK_AHEAD, n))]
        for g in range(n):
            s = pending.pop(0)
            if g + QK_AHEAD < n:
                pending.append(scores(k, g + QK_AHEAD))
            yield g, s

    def exact_tile(j):
        k, vts = operands(j)
        for g, s in issue_ahead(k):
            m = m_scr[g]
            mn = jnp.maximum(m, jnp.max(s, axis=0, keepdims=True))
            p = jnp.exp2(s - mn).astype(BF16)
            acc_scr[g] = acc_scr[g] * jnp.exp2(m - mn) + _mm(vts[vhead[g]], p)
            m_scr[g] = mn

    def lazy_tile(j, first):
        k, vts = operands(j)
        new, worst = [], None
        for g, s in issue_ahead(k):
            m = m_scr[g]
            top = jnp.max(s, axis=0, keepdims=True)
            p = jnp.exp2(s - m).astype(BF16)
            mn = jnp.maximum(m, top)
            new.append((mn, (acc_scr[g] + _mm(vts[vhead[g]], p)) * jnp.exp2(m - mn)))
            off = jnp.abs(top - m) if first else top - m
            worst = off if worst is None else jnp.maximum(worst, off)
        safe = jnp.max(worst) <= LAZY_MAX_RISE

        @pl.when(safe)
        def _():
            for g, (mn, acc) in enumerate(new):
                m_scr[g] = mn
                acc_scr[g] = acc

        @pl.when(jnp.logical_not(safe))
        def _():
            if first:
                m_scr[...] = jnp.full(m_scr.shape, NEG, F32)
            exact_tile(j)

    m_scr[...] = jnp.zeros(m_scr.shape, F32)
    acc_scr[...] = jnp.zeros(acc_scr.shape, F32)
    lazy_tile(0, first=True)

    def step(j, c):
        lazy_tile(j, first=False)
        return c

    lax.fori_loop(1, nk, step, 0)
    return [acc_scr[g] for g in range(n)]


def _normalised(acc):
    return acc[:HEAD_V] / acc[HEAD_V:HEAD_V + 1]


def _mla_body(q_ref, k_ref, vt_ref, o_ref, m_scr, acc_scr, *, nk):
    heads = list(range(MLA_HEADS))
    qs = [q_ref[0, :, h * LANES:(h + 1) * LANES] for h in heads]
    accs = _flash_chains(k_ref, vt_ref, qs, heads, heads, nk, m_scr, acc_scr)
    o_ref[0] = jnp.concatenate([_normalised(a) for a in accs], axis=0).T.astype(o_ref.dtype)


def _dense_attn_call(body, name, q, k, vt, extra, out_width, tq, chains):
    B, S, wq = q.shape
    nk, wv, tk = vt.shape[1:]
    return pl.pallas_call(
        body,
        out_shape=jax.ShapeDtypeStruct((B, S, out_width), BF16),
        grid=(B, S // tq),
        in_specs=[pl.BlockSpec((1, tq, wq), lambda b, i: (b, i, 0)),
                  pl.BlockSpec((1, S, wq), lambda b, i: (b, 0, 0)),
                  pl.BlockSpec((1, nk, wv, tk), lambda b, i: (b, 0, 0, 0)),
                  *[pl.BlockSpec(a.shape, lambda b, i: (0, 0)) for a in extra]],
        out_specs=pl.BlockSpec((1, tq, out_width), lambda b, i: (b, i, 0)),
        scratch_shapes=[pltpu.VMEM((chains, 1, tq), F32),
                        pltpu.VMEM((chains, HEAD_V + BF16_SUBLANES, tq), F32)],
        compiler_params=pltpu.CompilerParams(
            dimension_semantics=("parallel", "arbitrary"), vmem_limit_bytes=VMEM_LIMIT),
        name=name,
    )(q, k, vt, *extra)


def _mla_attn(q, k, vt, tq):
    return _dense_attn_call(functools.partial(_mla_body, nk=vt.shape[1]), "mla_attn", q, k, vt, (), MLA_WIDTH, tq,
                            MLA_HEADS)


def _diff_body(q_ref, k_ref, vt_ref, lam_ref, sg_ref, o_ref, m_scr, acc_scr, *, nk, lam_init):
    lane = lax.broadcasted_iota(jnp.int32, (1, LANES), 1)
    per_blk = LANES // DIFF_QK
    chains = list(range(2 * DIFF_HEADS))
    qs = []
    for g in chains:
        qb = q_ref[0, :, (g // per_blk) * LANES:(g // per_blk + 1) * LANES]
        lo = (g % per_blk) * DIFF_QK
        qs.append(jnp.where((lane >= lo) & (lane < lo + DIFF_QK), qb, jnp.zeros_like(qb)))
    accs = _flash_chains(k_ref, vt_ref, qs, [g // per_blk for g in chains], [g // 2 for g in chains], nk,
                         m_scr, acc_scr)

    lv = lam_ref[...]
    lam = (jnp.exp(jnp.sum(lv[0:1] * lv[1:2], axis=-1, keepdims=True))
           - jnp.exp(jnp.sum(lv[2:3] * lv[3:4], axis=-1, keepdims=True)) + lam_init)
    outs = []
    for h in range(DIFF_HEADS):
        o = _normalised(accs[2 * h]) - lam * _normalised(accs[2 * h + 1])
        r = lax.rsqrt(jnp.sum(o * o, axis=0, keepdims=True) * (1.0 / DIFF_V) + EPS)
        outs.append(o * r * (sg_ref[...] * (1.0 - lam_init)))
    o_ref[0] = jnp.concatenate(outs, axis=0).T.astype(o_ref.dtype)


def _diff_attn(q, k, vt, lamv, sgt, lam_init, tq):
    body = functools.partial(_diff_body, nk=vt.shape[1], lam_init=lam_init)
    return _dense_attn_call(body, "diff_attn", q, k, vt, (lamv, sgt), DIFF_WIDTH, tq, 2 * DIFF_HEADS)


def _na_body(q_ref, k_ref, vt_ref, bm_ref, o_ref, *, nrb):
    rb = pl.program_id(1)
    variant = jnp.where(rb == 0, 0, jnp.where(rb == nrb - 1, 2, 1))
    chunk0 = jnp.clip(rb - 1, 0, nrb - 3)
    tq = NA_RB * GRID_W
    off = pl.multiple_of(chunk0 * tq, tq)
    nkeys = NA_KROWS * GRID_W
    k = k_ref[0, pl.ds(off, nkeys), :]
    vt = jnp.concatenate([vt_ref[0, chunk0 + c] for c in range(NA_KROWS // NA_RB)], axis=1)
    ones = jnp.ones((BF16_SUBLANES, nkeys), BF16)
    lane = lax.broadcasted_iota(jnp.int32, (1, LANES), 1)
    per_blk = LANES // NA_DIM

    def scores(h):
        blk = slice((h // per_blk) * LANES, (h // per_blk + 1) * LANES)
        qb = q_ref[0, :, blk]
        lo = (h % per_blk) * NA_DIM
        qh = jnp.where((lane >= lo) & (lane < lo + NA_DIM), qb, jnp.zeros_like(qb))
        return _nt(k[:, blk], qh) + bm_ref[h, variant]

    def attend(shifted):
        pending = [scores(h) for h in range(min(QK_AHEAD, NA_HEADS))]
        outs, worst = [], None
        for h in range(NA_HEADS):
            s = pending.pop(0)
            if h + QK_AHEAD < NA_HEADS:
                pending.append(scores(h + QK_AHEAD))
            top = jnp.max(s, axis=0, keepdims=True)
            p = jnp.exp2(s - top if shifted else s).astype(BF16)
            vt_ext = jnp.concatenate([vt[h * NA_DIM:(h + 1) * NA_DIM], ones], axis=0)
            acc = _mm(vt_ext, p)
            outs.append(acc[:NA_DIM] / acc[NA_DIM:NA_DIM + 1])
            worst = jnp.abs(top) if worst is None else jnp.maximum(worst, jnp.abs(top))
        return jnp.concatenate(outs, axis=0).T.astype(o_ref.dtype), worst

    out, worst = attend(shifted=False)
    safe = jnp.max(worst) <= LAZY_MAX_RISE

    @pl.when(safe)
    def _():
        o_ref[0] = out

    @pl.when(jnp.logical_not(safe))
    def _():
        o_ref[0] = attend(shifted=True)[0]


def _na_attn(q, k, vt, bm):
    B, S, _ = q.shape
    tq = NA_RB * GRID_W
    nrb = S // tq
    return pl.pallas_call(
        functools.partial(_na_body, nrb=nrb),
        out_shape=jax.ShapeDtypeStruct((B, S, NA_WIDTH), BF16),
        grid=(B, nrb),
        in_specs=[pl.BlockSpec((1, tq, NA_WIDTH), lambda b, i: (b, i, 0)),
                  pl.BlockSpec((1, S, NA_WIDTH), lambda b, i: (b, 0, 0)),
                  pl.BlockSpec((1, nrb, NA_WIDTH, tq), lambda b, i: (b, 0, 0, 0)),
                  _resident(bm)],
        out_specs=pl.BlockSpec((1, tq, NA_WIDTH), lambda b, i: (b, i, 0)),
        compiler_params=pltpu.CompilerParams(
            dimension_semantics=("parallel", "arbitrary"), vmem_limit_bytes=VMEM_LIMIT),
        name="na_attn",
    )(q, k, vt, bm)


def _na_bias_table(rpb, rows):
    nrb = rows // NA_RB
    W = GRID_W
    H = rpb.shape[0]
    pad_c = W - NA_KC
    rp = jnp.pad(rpb.astype(F32), ((0, 0), (0, 0), (pad_c, pad_c + 1)))
    skew = jnp.tile(rp, (1, 1, W))[:, :, :W * (2 * W - 1)].reshape(H, -1, W, 2 * W - 1)
    t = skew[:, :, :, W - 1:].transpose(0, 2, 1, 3)
    pad_lo, pad_hi = NA_KR // 2, NA_KROWS - NA_KR
    t = jnp.pad(t, ((0, 0), (0, 0), (pad_lo, pad_hi), (0, 0)))
    c = np.arange(W)[:, None, None]
    kri = np.arange(NA_KROWS)[None, :, None]
    kc = np.arange(W)[None, None, :]
    c0 = np.clip(c - NA_KC // 2, 0, W - NA_KC)
    col_ok = (kc >= c0) & (kc < c0 + NA_KC)
    blocks = []
    for rb in (0, 1, nrb - 1):
        start = NA_RB * min(max(rb - 1, 0), nrb - 3)
        for ri in range(NA_RB):
            r = NA_RB * rb + ri
            r0 = min(max(r - NA_KR // 2, 0), rows - NA_KR)
            d0 = start - r + (NA_KR - 1) + pad_lo
            assert 0 <= d0 and d0 + NA_KROWS <= t.shape[2]
            ok = ((start + kri >= r0) & (start + kri < r0 + NA_KR)) & col_ok
            blk = jnp.where(ok[None], t[:, :, d0:d0 + NA_KROWS, :] * LOG2E, NEG)
            blocks.append(blk.reshape(H, W, NA_KROWS * W))
    table = jnp.stack(blocks, axis=1).reshape(H, 3, NA_RB * W, NA_KROWS * W)
    return table.transpose(0, 1, 3, 2)


def _outproj_body(x_ref, om_ref, on_ref, od_ref, gt_ref, gate_ref, w_ref, o_ref):
    def branch(o_ref_, a, b):
        mix = (o_ref_[0].astype(F32) * gt_ref[0, :, a:b].astype(F32)).astype(BF16)
        return _mm(mix, w_ref[a:b, :])

    y = (branch(om_ref, 0, MLA_WIDTH) + branch(on_ref, MLA_WIDTH, MLA_WIDTH + NA_WIDTH)
         + branch(od_ref, MLA_WIDTH + NA_WIDTH, D_MODEL))
    o_ref[0] = x_ref[0] + gate_ref[0, 2:3, :] * y


def _outproj(x, om, on, od, gt, mod_b, w_out, tm):
    B, S, _ = x.shape
    tok = lambda w: pl.BlockSpec((1, tm, w), lambda b, i: (b, i, 0))
    return pl.pallas_call(
        _outproj_body,
        out_shape=jax.ShapeDtypeStruct((B, S, D_MODEL), F32),
        grid=(B, S // tm),
        in_specs=[tok(D_MODEL), tok(MLA_WIDTH), tok(NA_WIDTH), tok(DIFF_WIDTH), tok(D_MODEL),
                  pl.BlockSpec((1, 3, D_MODEL), lambda b, i: (b, 0, 0)),
                  _resident(w_out)],
        out_specs=tok(D_MODEL),
        compiler_params=pltpu.CompilerParams(
            dimension_semantics=("parallel", "parallel"), vmem_limit_bytes=VMEM_LIMIT),
        name="outproj",
    )(x, om, on, od, gt, mod_b, w_out)


def _rot_cols(w, half):
    return jnp.concatenate([-w[..., half:], w[..., :half]], axis=-1)


def _swap_halves(g, half):
    return jnp.concatenate([g[..., half:], g[..., :half]], axis=-1)


def _pad_cols(w, left, total):
    return jnp.pad(w, ((0, 0), (left, total - left - w.shape[-1])))


def _prep_layer(w_in, w_uq, w_ukv, norm_g, q_lat_g, kv_lat_g, mla_q_g, mla_k_g,
                na_q_g, na_k_g, diff_q_g, diff_k_g):
    sizes = (Q_LORA, KV_LORA, MLA_ROPE, 3 * NA_WIDTH, 3 * DIFF_WIDTH, D_MODEL)
    splits = np.cumsum(sizes)[:-1].tolist()
    w_cq, w_ckv, w_kpe, w_na, w_diff, w_gate = jnp.split(w_in, splits, axis=-1)
    hr = MLA_ROPE // 2
    hd = DIFF_QK // 2
    w_qd, w_kd, w_vd = jnp.split(w_diff, 3, axis=-1)
    w_nq, w_nk, w_nv = jnp.split(w_na, 3, axis=-1)
    rot_d = lambda w: _rot_cols(w.reshape(D_MODEL, -1, DIFF_QK), hd).reshape(D_MODEL, DIFF_WIDTH)
    w_ext = jnp.concatenate([
        w_cq, w_ckv,
        _pad_cols(w_kpe, MLA_NOPE, LANES), _pad_cols(_rot_cols(w_kpe, hr), MLA_NOPE, LANES),
        w_nq, w_nk, w_qd, w_kd, rot_d(w_qd), rot_d(w_kd), w_gate], axis=-1).astype(BF16)
    wnvt = w_nv.T.astype(BF16)
    wdvt = w_vd.T.astype(BF16)

    uq = w_uq.reshape(Q_LORA, MLA_HEADS, MLA_QK)
    uq_raw = jnp.pad(uq, ((0, 0), (0, 0), (0, LANES - MLA_QK)))
    uq_rot = jnp.pad(_rot_cols(uq[..., MLA_NOPE:], hr), ((0, 0), (0, 0), (MLA_NOPE, LANES - MLA_QK)))
    wuq = jnp.concatenate([uq_raw.reshape(Q_LORA, -1), uq_rot.reshape(Q_LORA, -1)], axis=-1).astype(BF16)

    ukv = w_ukv.reshape(KV_LORA, MLA_HEADS, MLA_NOPE + MLA_V)
    wuk = jnp.pad(ukv[..., :MLA_NOPE], ((0, 0), (0, 0), (0, LANES - MLA_NOPE))).reshape(KV_LORA, -1).astype(BF16)
    wuvt = ukv[..., MLA_NOPE:].reshape(KV_LORA, -1).T.astype(BF16)

    def row(v):
        return jnp.pad(v.astype(F32), (0, D_MODEL - v.shape[0]))

    def mla_rows(g):
        rot = jnp.pad(_swap_halves(g[MLA_NOPE:], hr), (MLA_NOPE, 0))
        return row(g), row(rot)

    def diff_rows(g):
        n = DIFF_WIDTH // DIFF_QK
        return row(jnp.tile(g, n)), row(jnp.tile(_swap_halves(g, hd), n))

    rows = [row(norm_g), row(q_lat_g), row(kv_lat_g), *mla_rows(mla_q_g), *mla_rows(mla_k_g),
            row(jnp.tile(na_q_g, NA_HEADS)), row(jnp.tile(na_k_g, NA_HEADS)),
            *diff_rows(diff_q_g), *diff_rows(diff_k_g)]
    rows += [jnp.zeros((D_MODEL,), F32)] * (G_ROWS - len(rows))
    return w_ext, wuq, wuk, wuvt, wnvt, wdvt, jnp.stack(rows)


def _rope_tables(S):
    def cs(dim):
        inv = ROPE_THETA ** (-jnp.arange(0, dim, 2, dtype=F32) / dim)
        ang = jnp.arange(S, dtype=F32)[:, None] * inv[None, :]
        return jnp.cos(ang), jnp.sin(ang)

    c, s = cs(MLA_ROPE)
    pad = LANES - MLA_QK
    cm = jnp.concatenate([jnp.ones((S, MLA_NOPE), F32), c, c, jnp.zeros((S, pad), F32)], axis=-1)
    sm = jnp.concatenate([jnp.zeros((S, MLA_NOPE), F32), s, s, jnp.zeros((S, pad), F32)], axis=-1)
    c, s = cs(DIFF_QK)
    n = LANES // DIFF_QK
    cd = jnp.tile(jnp.concatenate([c, c], axis=-1), (1, n))
    sd = jnp.tile(jnp.concatenate([s, s], axis=-1), (1, n))
    return cm, sm, cd, sd


def _tile(S, pref):
    t = pref
    while S % t:
        t //= 2
    return t


def kernel(x, c, ada_w, ada_b, norm_g, w_in, q_lat_g, w_uq, kv_lat_g, w_ukv, mla_q_g, mla_k_g,
           na_q_g, na_k_g, na_rpb, diff_q_g, diff_k_g, lam_q1, lam_k1, lam_q2, lam_k2, subln_g, w_out):
    B, S, D = x.shape
    L = ada_w.shape[0]
    rows = S // GRID_W
    assert D == D_MODEL and S % (NA_RB * GRID_W) == 0 and rows >= NA_KROWS

    tabs = _rope_tables(S)
    mod = _adaln(c, ada_w, ada_b)
    tm = _tile(S, TILE_M)
    tk = _tile(S, TILE_K)
    tqd = _tile(S, TILE_Q_DIFF)

    weights = jax.vmap(_prep_layer)(w_in, w_uq, w_ukv, norm_g, q_lat_g, kv_lat_g, mla_q_g, mla_k_g,
                                    na_q_g, na_k_g, diff_q_g, diff_k_g)
    bias = jax.vmap(functools.partial(_na_bias_table, rows=rows))(na_rpb)
    mod_b = mod.transpose(0, 2, 1, 3)
    lamv = jnp.stack([lam_q1, lam_k1, lam_q2, lam_k2], axis=1).astype(F32)
    sgt = jnp.broadcast_to(subln_g.astype(F32)[:, :, None], (L, DIFF_V, tqd))
    w_out_b = w_out.astype(BF16)

    h = x
    for i in range(L):
        lam_init = 0.8 - 0.6 * math.exp(-0.3 * i)
        w_ext, wuq, wuk, wuvt, wnvt, wdvt, gv = (w[i] for w in weights)
        qm, km, vmt, nq, nk, nvt, dq, dk, dvt, gt = _inproj(h, mod_b[i], gv, w_ext, wuq, wuk, wuvt, wnvt, wdvt,
                                                            tabs, tm, tk)
        om = _mla_attn(qm, km, vmt, _tile(S, TILE_Q_MLA))
        on = _na_attn(nq, nk, nvt, bias[i])
        od = _diff_attn(dq, dk, dvt, lamv[i], sgt[i], lam_init, tqd)
        h = _outproj(h, om, on, od, gt, mod_b[i], w_out_b[i], tm)
    return h
```

```python
import functools
import math

import numpy as np
import jax
import jax.numpy as jnp
from jax import lax
from jax.experimental import pallas as pl
from jax.experimental.pallas import tpu as pltpu

F32 = jnp.float32
BF16 = jnp.bfloat16

D_MODEL = 1024
GRID_W = 64
EPS = 1e-6
ROPE_THETA = 10000.0
MLA_HEADS, MLA_NOPE, MLA_ROPE, MLA_V = 6, 64, 32, 64
MLA_QK = MLA_NOPE + MLA_ROPE
Q_LORA, KV_LORA = 256, 128
NA_HEADS, NA_DIM, NA_KR, NA_KC = 6, 64, 8, 16
DIFF_HEADS, DIFF_QK = 4, 32
DIFF_V = 2 * DIFF_QK
MLA_WIDTH, NA_WIDTH, DIFF_WIDTH = MLA_HEADS * MLA_V, NA_HEADS * NA_DIM, DIFF_HEADS * DIFF_V
HEAD_V = 64

LANES = 128
BF16_SUBLANES = 16
MXU_N = 256
VMEM_LIMIT = 52 * 1024 * 1024

C_CQ, C_CKV, C_KPE = 0, 256, 384
C_NA = 512
C_DIFF = 1280
C_GATE = 1792
N_EXT = 2816

G_NORM, G_QLAT, G_KVLAT, G_MQ, G_MQR, G_MK, G_MKR, G_NQ, G_NK, G_DQ, G_DQR, G_DK, G_DKR = range(13)
G_ROWS = 16

TILE_M = 1024
TILE_Q_MLA = 1024
TILE_Q_DIFF = 1024
TILE_K = 512
QK_AHEAD = 3

NA_RB = 4
NA_KROWS = 12
NEG = -1e30
LOG2E = math.log2(math.e)
LAZY_MAX_RISE = 64.0


def _nt(a, b):
    return lax.dot_general(a, b, (((1,), (1,)), ((), ())), preferred_element_type=F32)


def _mm(a, b):
    return jnp.dot(a, b, preferred_element_type=F32)


def _seg_ones(seg, width):
    sh = int(math.log2(seg))
    r = lax.broadcasted_iota(jnp.int32, (width, width), 0) >> sh
    c = lax.broadcasted_iota(jnp.int32, (width, width), 1) >> sh
    return jnp.where(r == c, 1.0, 0.0).astype(BF16)


def _seg_rms_scale(x, seg):
    x2 = (x * x).astype(BF16)
    width = x.shape[1]
    parts = []
    for a in range(0, width, MXU_N):
        w = min(MXU_N, width - a)
        parts.append(_mm(x2[:, a:a + w], _seg_ones(seg, w)))
    ss = parts[0] if len(parts) == 1 else jnp.concatenate(parts, axis=1)
    return lax.rsqrt(ss * (1.0 / seg) + EPS)


def _rotate_half(x, group):
    half = group // 2
    lane = lax.broadcasted_iota(jnp.int32, (1, LANES), 1)
    first = (lane & (group - 1)) < half
    from_above = pltpu.roll(x, LANES - half, 1)
    from_below = pltpu.roll(x, half, 1)
    return jnp.where(first, -from_above, from_below)


def _resident(a):
    zeros = (0,) * a.ndim
    return pl.BlockSpec(a.shape, lambda *_: zeros, pipeline_mode=pl.Buffered(1))


def _adaln_body(c_ref, w_ref, b_ref, o_ref):
    c = c_ref[...]
    a = c / (1.0 + jnp.exp(-c))
    o_ref[0, 0] = jnp.dot(a, w_ref[0], preferred_element_type=F32,
                          precision=lax.Precision.HIGHEST) + b_ref[0, 0]


def _adaln(c, ada_w, ada_b):
    L = ada_w.shape[0]
    B = c.shape[0]
    b4 = ada_b.reshape(L, 3, 1, D_MODEL)
    return pl.pallas_call(
        _adaln_body,
        out_shape=jax.ShapeDtypeStruct((L, 3, B, D_MODEL), F32),
        grid=(L, 3),
        in_specs=[
            pl.BlockSpec((B, D_MODEL), lambda l, j: (0, 0)),
            pl.BlockSpec((1, D_MODEL, D_MODEL), lambda l, j: (l, 0, j)),
            pl.BlockSpec((1, 1, 1, D_MODEL), lambda l, j: (l, j, 0, 0)),
        ],
        out_specs=pl.BlockSpec((1, 1, B, D_MODEL), lambda l, j: (l, j, 0, 0)),
        compiler_params=pltpu.CompilerParams(
            dimension_semantics=("arbitrary", "arbitrary"), vmem_limit_bytes=VMEM_LIMIT),
        name="adaln",
    )(c, ada_w, b4)


def _inproj_body(x_ref, mod_ref, gv_ref, w_ref, wuq_ref, wuk_ref, wuvt_ref, wnvt_ref, wdvt_ref,
                 cm_ref, sm_ref, cd_ref, sd_ref,
                 qm_ref, km_ref, vmt_ref, nq_ref, nk_ref, nvt_ref, dq_ref, dk_ref, dvt_ref, gt_ref):
    def gain(row, width, off=0):
        return gv_ref[row:row + 1, off:off + width]

    x = x_ref[0]
    ms = jnp.mean(x * x, axis=-1, keepdims=True)
    xn = x * lax.rsqrt(ms + EPS) * gain(G_NORM, D_MODEL)
    h = (xn * (1.0 + mod_ref[0, 1:2, :]) + mod_ref[0, 0:1, :]).astype(BF16)

    def proj(a, b):
        return _mm(h, w_ref[:, a:b])

    def lat_norm(v, row, width):
        r = lax.rsqrt(jnp.mean(v * v, axis=-1, keepdims=True) + EPS)
        return (v * r * gain(row, width)).astype(BF16)

    cm, sm = cm_ref[...], sm_ref[...]
    cd, sd = cd_ref[...], sd_ref[...]

    lat = proj(C_CQ, C_NA)
    cqn = lat_norm(lat[:, C_CQ:C_CQ + Q_LORA], G_QLAT, Q_LORA)
    qq = _mm(cqn, wuq_ref[...])
    wq_c = gain(G_MQ, LANES) * cm
    wq_s = gain(G_MQR, LANES) * sm
    q_scale = MLA_QK ** -0.5 * LOG2E
    for hd in range(MLA_HEADS):
        a = qq[:, hd * LANES:(hd + 1) * LANES]
        r = lax.rsqrt(jnp.sum(a * a, axis=-1, keepdims=True) * (1.0 / MLA_QK) + EPS)
        rotated = _rotate_half(a, MLA_ROPE)
        qm_ref[0, :, hd * LANES:(hd + 1) * LANES] = ((r * q_scale) * (a * wq_c + rotated * wq_s)).astype(BF16)

    ckvn = lat_norm(lat[:, C_CKV:C_CKV + KV_LORA], G_KVLAT, KV_LORA)
    _store_chunks(vmt_ref, _nt(wuvt_ref[...], ckvn).astype(BF16))
    kn = _mm(ckvn, wuk_ref[...])
    kpe = lat[:, C_KPE:C_KPE + LANES]
    wk_c = gain(G_MK, LANES) * cm
    k_rot = _rotate_half(kpe, MLA_ROPE) * (gain(G_MKR, LANES) * sm)
    for hd in range(MLA_HEADS):
        a = kn[:, hd * LANES:(hd + 1) * LANES] + kpe
        r = lax.rsqrt(jnp.sum(a * a, axis=-1, keepdims=True) * (1.0 / MLA_QK) + EPS)
        km_ref[0, :, hd * LANES:(hd + 1) * LANES] = (r * (a * wk_c + k_rot)).astype(BF16)

    nqk = proj(C_NA, C_NA + 2 * NA_WIDTH)
    na_scale = NA_DIM ** -0.5 * LOG2E
    a = nqk[:, :NA_WIDTH]
    nq_ref[0] = (a * (_seg_rms_scale(a, NA_DIM) * na_scale) * gain(G_NQ, NA_WIDTH)).astype(BF16)
    a = nqk[:, NA_WIDTH:]
    nk_ref[0] = (a * _seg_rms_scale(a, NA_DIM) * gain(G_NK, NA_WIDTH)).astype(BF16)
    _store_chunks(nvt_ref, _nt(wnvt_ref[...], h).astype(BF16))

    dd = proj(C_DIFF, C_DIFF + 2 * DIFF_WIDTH)
    d_scale = DIFF_QK ** -0.5 * LOG2E
    reps = DIFF_WIDTH // LANES
    cdw, sdw = jnp.concatenate([cd] * reps, axis=1), jnp.concatenate([sd] * reps, axis=1)
    for (o_ref, base, grow, grot, sc) in ((dq_ref, 0, G_DQ, G_DQR, d_scale), (dk_ref, DIFF_WIDTH, G_DK, G_DKR, 1.0)):
        a = dd[:, base:base + DIFF_WIDTH]
        b = jnp.concatenate([_rotate_half(a[:, i * LANES:(i + 1) * LANES], DIFF_QK) for i in range(reps)], axis=1)
        r = _seg_rms_scale(a, DIFF_QK)
        o_ref[0] = ((r * sc) * (a * (gain(grow, DIFF_WIDTH) * cdw) + b * (gain(grot, DIFF_WIDTH) * sdw))).astype(BF16)
    _store_chunks(dvt_ref, _nt(wdvt_ref[...], h).astype(BF16))

    g = proj(C_GATE, N_EXT)
    gt_ref[0] = (g / (1.0 + jnp.exp(-g))).astype(BF16)


def _store_chunks(ref, val):
    tk = ref.shape[3]
    for c in range(ref.shape[1]):
        ref[0, c] = val[:, c * tk:(c + 1) * tk]


def _inproj(x, mod_b, gv, w_ext, wuq, wuk, wuvt, wnvt, wdvt, tabs, tm, tk):
    B, S, _ = x.shape
    nt = S // tm
    cm, sm, cd, sd = tabs
    tok = lambda w: pl.BlockSpec((1, tm, w), lambda b, i: (b, i, 0))
    def tr(w, c):
        if c <= tm:
            return pl.BlockSpec((1, tm // c, w, c), lambda b, i: (b, i, 0, 0))
        per = c // tm
        return pl.BlockSpec((1, 1, w, tm), lambda b, i: (b, i // per, 0, i % per))
    tab = pl.BlockSpec((tm, LANES), lambda b, i: (i, 0))
    tok_out = lambda w: jax.ShapeDtypeStruct((B, S, w), BF16)
    tr_out = lambda w, c: jax.ShapeDtypeStruct((B, S // c, w, c), BF16)
    tna = NA_RB * GRID_W
    weights = (gv, w_ext, wuq, wuk, wuvt, wnvt, wdvt)
    return pl.pallas_call(
        _inproj_body,
        out_shape=[tok_out(MLA_HEADS * LANES), tok_out(MLA_HEADS * LANES), tr_out(MLA_WIDTH, tk),
                   tok_out(NA_WIDTH), tok_out(NA_WIDTH), tr_out(NA_WIDTH, tna),
                   tok_out(DIFF_WIDTH), tok_out(DIFF_WIDTH), tr_out(DIFF_WIDTH, tk), tok_out(D_MODEL)],
        grid=(B, nt),
        in_specs=[tok(D_MODEL), pl.BlockSpec((1, 3, D_MODEL), lambda b, i: (b, 0, 0)),
                  *[_resident(a) for a in weights], tab, tab, tab, tab],
        out_specs=[tok(MLA_HEADS * LANES), tok(MLA_HEADS * LANES), tr(MLA_WIDTH, tk),
                   tok(NA_WIDTH), tok(NA_WIDTH), tr(NA_WIDTH, tna),
                   tok(DIFF_WIDTH), tok(DIFF_WIDTH), tr(DIFF_WIDTH, tk), tok(D_MODEL)],
        compiler_params=pltpu.CompilerParams(
            dimension_semantics=("parallel", "parallel"), vmem_limit_bytes=VMEM_LIMIT),
        name="inproj",
    )(x, mod_b, *weights, cm, sm, cd, sd)


def _flash_chains(k_ref, vt_ref, qs, kblk, vhead, nk, m_scr, acc_scr):
    tk = vt_ref.shape[3]
    n = len(qs)

    def operands(j):
        off = j * tk if isinstance(j, int) else pl.multiple_of(j * tk, tk)
        k = k_ref[0, pl.ds(off, tk), :]
        vt = vt_ref[0, j]
        ones = jnp.ones((BF16_SUBLANES, tk), BF16)
        vts = {h: jnp.concatenate([vt[h * HEAD_V:(h + 1) * HEAD_V], ones], axis=0) for h in set(vhead)}
        return k, vts

    def scores(k, g):
        return _nt(k[:, kblk[g] * LANES:(kblk[g] + 1) * LANES], qs[g])

    def issue_ahead(k):
        pending = [scores(k, g) for g in range(min(QK_AHEAD, n))]
        for g in range(n):
            s = pending.pop(0)
            if g + QK_AHEAD < n:
                pending.append(scores(k, g + QK_AHEAD))
            yield g, s

    def exact_tile(j):
        k, vts = operands(j)
        for g, s in issue_ahead(k):
            m = m_scr[g]
            mn = jnp.maximum(m, jnp.max(s, axis=0, keepdims=True))
            p = jnp.exp2(s - mn).astype(BF16)
            acc_scr[g] = acc_scr[g] * jnp.exp2(m - mn) + _mm(vts[vhead[g]], p)
            m_scr[g] = mn

    def lazy_tile(j, first):
        k, vts = operands(j)
        new, worst = [], None
        for g, s in issue_ahead(k):
            m = m_scr[g]
            top = jnp.max(s, axis=0, keepdims=True)
            p = jnp.exp2(s - m).astype(BF16)
            mn = jnp.maximum(m, top)
            new.append((mn, (acc_scr[g] + _mm(vts[vhead[g]], p)) * jnp.exp2(m - mn)))
            off = jnp.abs(top - m) if first else top - m
            worst = off if worst is None else jnp.maximum(worst, off)
        safe = jnp.max(worst) <= LAZY_MAX_RISE

        @pl.when(safe)
        def _():
            for g, (mn, acc) in enumerate(new):
                m_scr[g] = mn
                acc_scr[g] = acc

        @pl.when(jnp.logical_not(safe))
        def _():
            if first:
                m_scr[...] = jnp.full(m_scr.shape, NEG, F32)
            exact_tile(j)

    m_scr[...] = jnp.zeros(m_scr.shape, F32)
    acc_scr[...] = jnp.zeros(acc_scr.shape, F32)
    lazy_tile(0, first=True)

    def step(j, c):
        lazy_tile(j, first=False)
        return c

    lax.fori_loop(1, nk, step, 0)
    return [acc_scr[g] for g in range(n)]


def _normalised(acc):
    return acc[:HEAD_V] / acc[HEAD_V:HEAD_V + 1]


def _mla_body(q_ref, k_ref, vt_ref, o_ref, m_scr, acc_scr, *, nk):
    heads = list(range(MLA_HEADS))
    qs = [q_ref[0, :, h * LANES:(h + 1) * LANES] for h in heads]
    accs = _flash_chains(k_ref, vt_ref, qs, heads, heads, nk, m_scr, acc_scr)
    o_ref[0] = jnp.concatenate([_normalised(a) for a in accs], axis=0).T.astype(o_ref.dtype)


def _dense_attn_call(body, name, q, k, vt, extra, out_width, tq, chains):
    B, S, wq = q.shape
    nk, wv, tk = vt.shape[1:]
    return pl.pallas_call(
        body,
        out_shape=jax.ShapeDtypeStruct((B, S, out_width), BF16),
        grid=(B, S // tq),
        in_specs=[pl.BlockSpec((1, tq, wq), lambda b, i: (b, i, 0)),
                  pl.BlockSpec((1, S, wq), lambda b, i: (b, 0, 0)),
                  pl.BlockSpec((1, nk, wv, tk), lambda b, i: (b, 0, 0, 0)),
                  *[pl.BlockSpec(a.shape, lambda b, i: (0, 0)) for a in extra]],
        out_specs=pl.BlockSpec((1, tq, out_width), lambda b, i: (b, i, 0)),
        scratch_shapes=[pltpu.VMEM((chains, 1, tq), F32),
                        pltpu.VMEM((chains, HEAD_V + BF16_SUBLANES, tq), F32)],
        compiler_params=pltpu.CompilerParams(
            dimension_semantics=("parallel", "arbitrary"), vmem_limit_bytes=VMEM_LIMIT),
        name=name,
    )(q, k, vt, *extra)


def _mla_attn(q, k, vt, tq):
    return _dense_attn_call(functools.partial(_mla_body, nk=vt.shape[1]), "mla_attn", q, k, vt, (), MLA_WIDTH, tq,
                            MLA_HEADS)


def _diff_body(q_ref, k_ref, vt_ref, lam_ref, sg_ref, o_ref, m_scr, acc_scr, *, nk, lam_init):
    lane = lax.broadcasted_iota(jnp.int32, (1, LANES), 1)
    per_blk = LANES // DIFF_QK
    chains = list(range(2 * DIFF_HEADS))
    qs = []
    for g in chains:
        qb = q_ref[0, :, (g // per_blk) * LANES:(g // per_blk + 1) * LANES]
        lo = (g % per_blk) * DIFF_QK
        qs.append(jnp.where((lane >= lo) & (lane < lo + DIFF_QK), qb, jnp.zeros_like(qb)))
    accs = _flash_chains(k_ref, vt_ref, qs, [g // per_blk for g in chains], [g // 2 for g in chains], nk,
                         m_scr, acc_scr)

    lv = lam_ref[...]
    lam = (jnp.exp(jnp.sum(lv[0:1] * lv[1:2], axis=-1, keepdims=True))
           - jnp.exp(jnp.sum(lv[2:3] * lv[3:4], axis=-1, keepdims=True)) + lam_init)
    outs = []
    for h in range(DIFF_HEADS):
        o = _normalised(accs[2 * h]) - lam * _normalised(accs[2 * h + 1])
        r = lax.rsqrt(jnp.sum(o * o, axis=0, keepdims=True) * (1.0 / DIFF_V) + EPS)
        outs.append(o * r * (sg_ref[...] * (1.0 - lam_init)))
    o_ref[0] = jnp.concatenate(outs, axis=0).T.astype(o_ref.dtype)


def _diff_attn(q, k, vt, lamv, sgt, lam_init, tq):
    body = functools.partial(_diff_body, nk=vt.shape[1], lam_init=lam_init)
    return _dense_attn_call(body, "diff_attn", q, k, vt, (lamv, sgt), DIFF_WIDTH, tq, 2 * DIFF_HEADS)


def _na_body(q_ref, k_ref, vt_ref, bm_ref, o_ref, *, nrb):
    rb = pl.program_id(1)
    variant = jnp.where(rb == 0, 0, jnp.where(rb == nrb - 1, 2, 1))
    chunk0 = jnp.clip(rb - 1, 0, nrb - 3)
    tq = NA_RB * GRID_W
    off = pl.multiple_of(chunk0 * tq, tq)
    nkeys = NA_KROWS * GRID_W
    k = k_ref[0, pl.ds(off, nkeys), :]
    vt = jnp.concatenate([vt_ref[0, chunk0 + c] for c in range(NA_KROWS // NA_RB)], axis=1)
    ones = jnp.ones((BF16_SUBLANES, nkeys), BF16)
    lane = lax.broadcasted_iota(jnp.int32, (1, LANES), 1)
    per_blk = LANES // NA_DIM

    def scores(h):
        blk = slice((h // per_blk) * LANES, (h // per_blk + 1) * LANES)
        qb = q_ref[0, :, blk]
        lo = (h % per_blk) * NA_DIM
        qh = jnp.where((lane >= lo) & (lane < lo + NA_DIM), qb, jnp.zeros_like(qb))
        return _nt(k[:, blk], qh) + bm_ref[h, variant]

    def attend(shifted):
        pending = [scores(h) for h in range(min(QK_AHEAD, NA_HEADS))]
        outs, worst = [], None
        for h in range(NA_HEADS):
            s = pending.pop(0)
            if h + QK_AHEAD < NA_HEADS:
                pending.append(scores(h + QK_AHEAD))
            top = jnp.max(s, axis=0, keepdims=True)
            p = jnp.exp2(s - top if shifted else s).astype(BF16)
            vt_ext = jnp.concatenate([vt[h * NA_DIM:(h + 1) * NA_DIM], ones], axis=0)
            acc = _mm(vt_ext, p)
            outs.append(acc[:NA_DIM] / acc[NA_DIM:NA_DIM + 1])
            worst = jnp.abs(top) if worst is None else jnp.maximum(worst, jnp.abs(top))
        return jnp.concatenate(outs, axis=0).T.astype(o_ref.dtype), worst

    out, worst = attend(shifted=False)
    safe = jnp.max(worst) <= LAZY_MAX_RISE

    @pl.when(safe)
    def _():
        o_ref[0] = out

    @pl.when(jnp.logical_not(safe))
    def _():
        o_ref[0] = attend(shifted=True)[0]


def _na_attn(q, k, vt, bm):
    B, S, _ = q.shape
    tq = NA_RB * GRID_W
    nrb = S // tq
    return pl.pallas_call(
        functools.partial(_na_body, nrb=nrb),
        out_shape=jax.ShapeDtypeStruct((B, S, NA_WIDTH), BF16),
        grid=(B, nrb),
        in_specs=[pl.BlockSpec((1, tq, NA_WIDTH), lambda b, i: (b, i, 0)),
                  pl.BlockSpec((1, S, NA_WIDTH), lambda b, i: (b, 0, 0)),
                  pl.BlockSpec((1, nrb, NA_WIDTH, tq), lambda b, i: (b, 0, 0, 0)),
                  _resident(bm)],
        out_specs=pl.BlockSpec((1, tq, NA_WIDTH), lambda b, i: (b, i, 0)),
        compiler_params=pltpu.CompilerParams(
            dimension_semantics=("parallel", "arbitrary"), vmem_limit_bytes=VMEM_LIMIT),
        name="na_attn",
    )(q, k, vt, bm)


def _na_bias_table(rpb, rows):
    nrb = rows // NA_RB
    W = GRID_W
    H = rpb.shape[0]
    pad_c = W - NA_KC
    rp = jnp.pad(jnp.flip(rpb.astype(F32), axis=-1), ((0, 0), (0, 0), (pad_c, pad_c + 1)))
    skew = jnp.tile(rp, (1, 1, W))[:, :, :W * (2 * W - 1)].reshape(H, -1, W, 2 * W - 1)
    pad_lo, pad_hi = NA_KR // 2, NA_KROWS - NA_KR
    t = jnp.pad(skew[:, :, :, W - 1:] * LOG2E, ((0, 0), (pad_lo, pad_hi), (0, 0), (0, 0)))
    kri = np.arange(NA_KROWS)[:, None, None]
    kc = np.arange(W)[None, :, None]
    c = np.arange(W)[None, None, :]
    c0 = np.clip(c - NA_KC // 2, 0, W - NA_KC)
    col_ok = (kc >= c0) & (kc < c0 + NA_KC)
    variants = []
    for rb in (0, 1, nrb - 1):
        start = NA_RB * min(max(rb - 1, 0), nrb - 3)
        per_row = []
        for ri in range(NA_RB):
            r = NA_RB * rb + ri
            r0 = min(max(r - NA_KR // 2, 0), rows - NA_KR)
            d0 = start - r + (NA_KR - 1) + pad_lo
            assert 0 <= d0 and d0 + NA_KROWS <= t.shape[1]
            ok = ((start + kri >= r0) & (start + kri < r0 + NA_KR)) & col_ok
            per_row.append(jnp.where(ok[None], t[:, d0:d0 + NA_KROWS], NEG))
        variants.append(jnp.concatenate(per_row, axis=-1))
    return jnp.stack(variants, axis=1).reshape(H, 3, NA_KROWS * W, NA_RB * W)


def _outproj_body(x_ref, om_ref, on_ref, od_ref, gt_ref, gate_ref, w_ref, o_ref):
    def branch(o_ref_, a, b):
        mix = (o_ref_[0].astype(F32) * gt_ref[0, :, a:b].astype(F32)).astype(BF16)
        return _mm(mix, w_ref[a:b, :])

    y = (branch(om_ref, 0, MLA_WIDTH) + branch(on_ref, MLA_WIDTH, MLA_WIDTH + NA_WIDTH)
         + branch(od_ref, MLA_WIDTH + NA_WIDTH, D_MODEL))
    o_ref[0] = x_ref[0] + gate_ref[0, 2:3, :] * y


def _outproj(x, om, on, od, gt, mod_b, w_out, tm):
    B, S, _ = x.shape
    tok = lambda w: pl.BlockSpec((1, tm, w), lambda b, i: (b, i, 0))
    return pl.pallas_call(
        _outproj_body,
        out_shape=jax.ShapeDtypeStruct((B, S, D_MODEL), F32),
        grid=(B, S // tm),
        in_specs=[tok(D_MODEL), tok(MLA_WIDTH), tok(NA_WIDTH), tok(DIFF_WIDTH), tok(D_MODEL),
                  pl.BlockSpec((1, 3, D_MODEL), lambda b, i: (b, 0, 0)),
                  _resident(w_out)],
        out_specs=tok(D_MODEL),
        compiler_params=pltpu.CompilerParams(
            dimension_semantics=("parallel", "parallel"), vmem_limit_bytes=VMEM_LIMIT),
        name="outproj",
    )(x, om, on, od, gt, mod_b, w_out)


def _swap_halves(g, half):
    return jnp.concatenate([g[..., half:], g[..., :half]], axis=-1)


def _pad_cols(w, left, total):
    return jnp.pad(w, ((0, 0), (left, total - left - w.shape[-1])))


def _prep_layer(w_in, w_uq, w_ukv, norm_g, q_lat_g, kv_lat_g, mla_q_g, mla_k_g,
                na_q_g, na_k_g, diff_q_g, diff_k_g):
    sizes = (Q_LORA, KV_LORA, MLA_ROPE, 3 * NA_WIDTH, 3 * DIFF_WIDTH, D_MODEL)
    splits = np.cumsum(sizes)[:-1].tolist()
    w_cq, w_ckv, w_kpe, w_na, w_diff, w_gate = jnp.split(w_in, splits, axis=-1)
    hr = MLA_ROPE // 2
    hd = DIFF_QK // 2
    w_qd, w_kd, w_vd = jnp.split(w_diff, 3, axis=-1)
    w_nq, w_nk, w_nv = jnp.split(w_na, 3, axis=-1)
    w_ext = jnp.concatenate([
        w_cq, w_ckv,
        _pad_cols(w_kpe, MLA_NOPE, LANES), w_nq, w_nk, w_qd, w_kd, w_gate], axis=-1).astype(BF16)
    wnvt = w_nv.T.astype(BF16)
    wdvt = w_vd.T.astype(BF16)

    uq = w_uq.reshape(Q_LORA, MLA_HEADS, MLA_QK)
    wuq = jnp.pad(uq, ((0, 0), (0, 0), (0, LANES - MLA_QK))).reshape(Q_LORA, -1).astype(BF16)

    ukv = w_ukv.reshape(KV_LORA, MLA_HEADS, MLA_NOPE + MLA_V)
    wuk = jnp.pad(ukv[..., :MLA_NOPE], ((0, 0), (0, 0), (0, LANES - MLA_NOPE))).reshape(KV_LORA, -1).astype(BF16)
    wuvt = ukv[..., MLA_NOPE:].reshape(KV_LORA, -1).T.astype(BF16)

    def row(v):
        return jnp.pad(v.astype(F32), (0, D_MODEL - v.shape[0]))

    def mla_rows(g):
        rot = jnp.pad(_swap_halves(g[MLA_NOPE:], hr), (MLA_NOPE, 0))
        return row(g), row(rot)

    def diff_rows(g):
        n = DIFF_WIDTH // DIFF_QK
        return row(jnp.tile(g, n)), row(jnp.tile(_swap_halves(g, hd), n))

    rows = [row(norm_g), row(q_lat_g), row(kv_lat_g), *mla_rows(mla_q_g), *mla_rows(mla_k_g),
            row(jnp.tile(na_q_g, NA_HEADS)), row(jnp.tile(na_k_g, NA_HEADS)),
            *diff_rows(diff_q_g), *diff_rows(diff_k_g)]
    rows += [jnp.zeros((D_MODEL,), F32)] * (G_ROWS - len(rows))
    return w_ext, wuq, wuk, wuvt, wnvt, wdvt, jnp.stack(rows)


def _rope_tables(S):
    def cs(dim):
        inv = ROPE_THETA ** (-jnp.arange(0, dim, 2, dtype=F32) / dim)
        ang = jnp.arange(S, dtype=F32)[:, None] * inv[None, :]
        return jnp.cos(ang), jnp.sin(ang)

    c, s = cs(MLA_ROPE)
    pad = LANES - MLA_QK
    cm = jnp.concatenate([jnp.ones((S, MLA_NOPE), F32), c, c, jnp.zeros((S, pad), F32)], axis=-1)
    sm = jnp.concatenate([jnp.zeros((S, MLA_NOPE), F32), s, s, jnp.zeros((S, pad), F32)], axis=-1)
    c, s = cs(DIFF_QK)
    n = LANES // DIFF_QK
    cd = jnp.tile(jnp.concatenate([c, c], axis=-1), (1, n))
    sd = jnp.tile(jnp.concatenate([s, s], axis=-1), (1, n))
    return cm, sm, cd, sd


def _tile(S, pref):
    t = pref
    while S % t:
        t //= 2
    return t


def kernel(x, c, ada_w, ada_b, norm_g, w_in, q_lat_g, w_uq, kv_lat_g, w_ukv, mla_q_g, mla_k_g,
           na_q_g, na_k_g, na_rpb, diff_q_g, diff_k_g, lam_q1, lam_k1, lam_q2, lam_k2, subln_g, w_out):
    B, S, D = x.shape
    L = ada_w.shape[0]
    rows = S // GRID_W
    assert D == D_MODEL and S % (NA_RB * GRID_W) == 0 and rows >= NA_KROWS

    tabs = _rope_tables(S)
    mod = _adaln(c, ada_w, ada_b)
    tm = _tile(S, TILE_M)
    tk = _tile(S, TILE_K)
    tqd = _tile(S, TILE_Q_DIFF)

    weights = jax.vmap(_prep_layer)(w_in, w_uq, w_ukv, norm_g, q_lat_g, kv_lat_g, mla_q_g, mla_k_g,
                                    na_q_g, na_k_g, diff_q_g, diff_k_g)
    bias = jax.vmap(functools.partial(_na_bias_table, rows=rows))(na_rpb)
    mod_b = mod.transpose(0, 2, 1, 3)
    lamv = jnp.stack([lam_q1, lam_k1, lam_q2, lam_k2], axis=1).astype(F32)
    sgt = jnp.broadcast_to(subln_g.astype(F32)[:, :, None], (L, DIFF_V, tqd))
    w_out_b = w_out.astype(BF16)

    h = x
    for i in range(L):
        lam_init = 0.8 - 0.6 * math.exp(-0.3 * i)
        w_ext, wuq, wuk, wuvt, wnvt, wdvt, gv = (w[i] for w in weights)
        qm, km, vmt, nq, nk, nvt, dq, dk, dvt, gt = _inproj(h, mod_b[i], gv, w_ext, wuq, wuk, wuvt, wnvt, wdvt,
                                                            tabs, tm, tk)
        om = _mla_attn(qm, km, vmt, _tile(S, TILE_Q_MLA))
        on = _na_attn(nq, nk, nvt, bias[i])
        od = _diff_attn(dq, dk, dvt, lamv[i], sgt[i], lam_init, tqd)
        h = _outproj(h, om, on, od, gt, mod_b[i], w_out_b[i], tm)
    return h
```

```python
import functools
import math

import numpy as np
import jax
import jax.numpy as jnp
from jax import lax
from jax.experimental import pallas as pl
from jax.experimental.pallas import tpu as pltpu

F32 = jnp.float32
BF16 = jnp.bfloat16

D_MODEL = 1024
GRID_W = 64
EPS = 1e-6
ROPE_THETA = 10000.0
MLA_HEADS, MLA_NOPE, MLA_ROPE, MLA_V = 6, 64, 32, 64
MLA_QK = MLA_NOPE + MLA_ROPE
Q_LORA, KV_LORA = 256, 128
NA_HEADS, NA_DIM, NA_KR, NA_KC = 6, 64, 8, 16
DIFF_HEADS, DIFF_QK = 4, 32
DIFF_V = 2 * DIFF_QK
MLA_WIDTH, NA_WIDTH, DIFF_WIDTH = MLA_HEADS * MLA_V, NA_HEADS * NA_DIM, DIFF_HEADS * DIFF_V
HEAD_V = 64

LANES = 128
BF16_SUBLANES = 16
MXU_N = 256
VMEM_LIMIT = 52 * 1024 * 1024

C_CQ, C_CKV, C_KPE = 0, 256, 384
C_NA = 512
C_DIFF = 1280
C_GATE = 1792
N_EXT = 2816

G_NORM, G_QLAT, G_KVLAT, G_MQ, G_MQR, G_MK, G_MKR, G_NQ, G_NK, G_DQ, G_DQR, G_DK, G_DKR = range(13)
G_ROWS = 16

TILE_M = 1024
TILE_Q_MLA = 1024
TILE_Q_DIFF = 1024
TILE_K = 512
QK_AHEAD = 3

NA_RB = 4
NA_KROWS = 12
NEG = -1e30
LOG2E = math.log2(math.e)
LAZY_MAX_RISE = 64.0


def _nt(a, b):
    return lax.dot_general(a, b, (((1,), (1,)), ((), ())), preferred_element_type=F32)


def _mm(a, b):
    return jnp.dot(a, b, preferred_element_type=F32)


def _seg_ones(seg, width):
    sh = int(math.log2(seg))
    r = lax.broadcasted_iota(jnp.int32, (width, width), 0) >> sh
    c = lax.broadcasted_iota(jnp.int32, (width, width), 1) >> sh
    return jnp.where(r == c, 1.0, 0.0).astype(BF16)


def _seg_rms_scale(x, seg):
    x2 = (x * x).astype(BF16)
    width = x.shape[1]
    parts = []
    for a in range(0, width, MXU_N):
        w = min(MXU_N, width - a)
        parts.append(_mm(x2[:, a:a + w], _seg_ones(seg, w)))
    ss = parts[0] if len(parts) == 1 else jnp.concatenate(parts, axis=1)
    return lax.rsqrt(ss * (1.0 / seg) + EPS)


def _rotate_half(x, group):
    half = group // 2
    lane = lax.broadcasted_iota(jnp.int32, (1, LANES), 1)
    first = (lane & (group - 1)) < half
    from_above = pltpu.roll(x, LANES - half, 1)
    from_below = pltpu.roll(x, half, 1)
    return jnp.where(first, -from_above, from_below)


def _resident(a):
    zeros = (0,) * a.ndim
    return pl.BlockSpec(a.shape, lambda *_: zeros, pipeline_mode=pl.Buffered(1))


def _adaln_body(c_ref, w_ref, b_ref, o_ref):
    c = c_ref[...]
    a = c / (1.0 + jnp.exp(-c))
    o_ref[0, 0] = jnp.dot(a, w_ref[0], preferred_element_type=F32,
                          precision=lax.Precision.HIGHEST) + b_ref[0, 0]


def _adaln(c, ada_w, ada_b):
    L = ada_w.shape[0]
    B = c.shape[0]
    b4 = ada_b.reshape(L, 3, 1, D_MODEL)
    return pl.pallas_call(
        _adaln_body,
        out_shape=jax.ShapeDtypeStruct((L, 3, B, D_MODEL), F32),
        grid=(L, 3),
        in_specs=[
            pl.BlockSpec((B, D_MODEL), lambda l, j: (0, 0)),
            pl.BlockSpec((1, D_MODEL, D_MODEL), lambda l, j: (l, 0, j)),
            pl.BlockSpec((1, 1, 1, D_MODEL), lambda l, j: (l, j, 0, 0)),
        ],
        out_specs=pl.BlockSpec((1, 1, B, D_MODEL), lambda l, j: (l, j, 0, 0)),
        compiler_params=pltpu.CompilerParams(
            dimension_semantics=("arbitrary", "arbitrary"), vmem_limit_bytes=VMEM_LIMIT),
        name="adaln",
    )(c, ada_w, b4)


def _inproj_body(x_ref, mod_ref, gv_ref, w_ref, wuq_ref, wuk_ref, wuvt_ref, wnvt_ref, wdvt_ref,
                 cm_ref, sm_ref, cd_ref, sd_ref,
                 qm_ref, km_ref, vmt_ref, nq_ref, nk_ref, nvt_ref, dq_ref, dk_ref, dvt_ref, gt_ref):
    def gain(row, width, off=0):
        return gv_ref[row:row + 1, off:off + width]

    x = x_ref[0]
    ms = jnp.mean(x * x, axis=-1, keepdims=True)
    xn = x * lax.rsqrt(ms + EPS) * gain(G_NORM, D_MODEL)
    h = (xn * (1.0 + mod_ref[0, 1:2, :]) + mod_ref[0, 0:1, :]).astype(BF16)

    def proj(a, b):
        return _mm(h, w_ref[:, a:b])

    def lat_norm(v, row, width):
        r = lax.rsqrt(jnp.mean(v * v, axis=-1, keepdims=True) + EPS)
        return (v * r * gain(row, width)).astype(BF16)

    cm, sm = cm_ref[...], sm_ref[...]
    cd, sd = cd_ref[...], sd_ref[...]

    lat = proj(C_CQ, C_NA)
    cqn = lat_norm(lat[:, C_CQ:C_CQ + Q_LORA], G_QLAT, Q_LORA)
    qq = _mm(cqn, wuq_ref[...])
    wq_c = gain(G_MQ, LANES) * cm
    wq_s = gain(G_MQR, LANES) * sm
    q_scale = MLA_QK ** -0.5 * LOG2E
    for hd in range(MLA_HEADS):
        a = qq[:, hd * LANES:(hd + 1) * LANES]
        r = lax.rsqrt(jnp.sum(a * a, axis=-1, keepdims=True) * (1.0 / MLA_QK) + EPS)
        rotated = _rotate_half(a, MLA_ROPE)
        qm_ref[0, :, hd * LANES:(hd + 1) * LANES] = ((r * q_scale) * (a * wq_c + rotated * wq_s)).astype(BF16)

    ckvn = lat_norm(lat[:, C_CKV:C_CKV + KV_LORA], G_KVLAT, KV_LORA)
    _store_chunks(vmt_ref, _nt(wuvt_ref[...], ckvn).astype(BF16))
    kn = _mm(ckvn, wuk_ref[...])
    kpe = lat[:, C_KPE:C_KPE + LANES]
    wk_c = gain(G_MK, LANES) * cm
    k_rot = _rotate_half(kpe, MLA_ROPE) * (gain(G_MKR, LANES) * sm)
    for hd in range(MLA_HEADS):
        a = kn[:, hd * LANES:(hd + 1) * LANES] + kpe
        r = lax.rsqrt(jnp.sum(a * a, axis=-1, keepdims=True) * (1.0 / MLA_QK) + EPS)
        km_ref[0, :, hd * LANES:(hd + 1) * LANES] = (r * (a * wk_c + k_rot)).astype(BF16)

    nqk = proj(C_NA, C_NA + 2 * NA_WIDTH)
    na_scale = NA_DIM ** -0.5 * LOG2E
    a = nqk[:, :NA_WIDTH]
    nq_ref[0] = (a * (_seg_rms_scale(a, NA_DIM) * na_scale) * gain(G_NQ, NA_WIDTH)).astype(BF16)
    a = nqk[:, NA_WIDTH:]
    nk_ref[0] = (a * _seg_rms_scale(a, NA_DIM) * gain(G_NK, NA_WIDTH)).astype(BF16)
    _store_chunks(nvt_ref, _nt(wnvt_ref[...], h).astype(BF16))

    dd = proj(C_DIFF, C_DIFF + 2 * DIFF_WIDTH)
    d_scale = DIFF_QK ** -0.5 * LOG2E
    reps = DIFF_WIDTH // LANES
    cdw, sdw = jnp.concatenate([cd] * reps, axis=1), jnp.concatenate([sd] * reps, axis=1)
    for (o_ref, base, grow, grot, sc) in ((dq_ref, 0, G_DQ, G_DQR, d_scale), (dk_ref, DIFF_WIDTH, G_DK, G_DKR, 1.0)):
        a = dd[:, base:base + DIFF_WIDTH]
        b = jnp.concatenate([_rotate_half(a[:, i * LANES:(i + 1) * LANES], DIFF_QK) for i in range(reps)], axis=1)
        r = _seg_rms_scale(a, DIFF_QK)
        o_ref[0] = ((r * sc) * (a * (gain(grow, DIFF_WIDTH) * cdw) + b * (gain(grot, DIFF_WIDTH) * sdw))).astype(BF16)
    _store_chunks(dvt_ref, _nt(wdvt_ref[...], h).astype(BF16))

    g = proj(C_GATE, N_EXT)
    gt_ref[0] = (g / (1.0 + jnp.exp(-g))).astype(BF16)


def _store_chunks(ref, val):
    tk = ref.shape[3]
    for c in range(ref.shape[1]):
        ref[0, c] = val[:, c * tk:(c + 1) * tk]


def _inproj(x, mod_b, gv, w_ext, wuq, wuk, wuvt, wnvt, wdvt, tabs, tm, tk):
    B, S, _ = x.shape
    nt = S // tm
    cm, sm, cd, sd = tabs
    tok = lambda w: pl.BlockSpec((1, tm, w), lambda b, i: (b, i, 0))
    def tr(w, c):
        if c <= tm:
            return pl.BlockSpec((1, tm // c, w, c), lambda b, i: (b, i, 0, 0))
        per = c // tm
        return pl.BlockSpec((1, 1, w, tm), lambda b, i: (b, i // per, 0, i % per))
    tab = pl.BlockSpec((tm, LANES), lambda b, i: (i, 0))
    tok_out = lambda w: jax.ShapeDtypeStruct((B, S, w), BF16)
    tr_out = lambda w, c: jax.ShapeDtypeStruct((B, S // c, w, c), BF16)
    tna = NA_RB * GRID_W
    weights = (gv, w_ext, wuq, wuk, wuvt, wnvt, wdvt)
    return pl.pallas_call(
        _inproj_body,
        out_shape=[tok_out(MLA_HEADS * LANES), tok_out(MLA_HEADS * LANES), tr_out(MLA_WIDTH, tk),
                   tok_out(NA_WIDTH), tok_out(NA_WIDTH), tr_out(NA_WIDTH, tna),
                   tok_out(DIFF_WIDTH), tok_out(DIFF_WIDTH), tr_out(DIFF_WIDTH, tk), tok_out(D_MODEL)],
        grid=(B, nt),
        in_specs=[tok(D_MODEL), pl.BlockSpec((1, 3, D_MODEL), lambda b, i: (b, 0, 0)),
                  *[_resident(a) for a in weights], tab, tab, tab, tab],
        out_specs=[tok(MLA_HEADS * LANES), tok(MLA_HEADS * LANES), tr(MLA_WIDTH, tk),
                   tok(NA_WIDTH), tok(NA_WIDTH), tr(NA_WIDTH, tna),
                   tok(DIFF_WIDTH), tok(DIFF_WIDTH), tr(DIFF_WIDTH, tk), tok(D_MODEL)],
        compiler_params=pltpu.CompilerParams(
            dimension_semantics=("parallel", "parallel"), vmem_limit_bytes=VMEM_LIMIT),
        name="inproj",
    )(x, mod_b, *weights, cm, sm, cd, sd)


def _flash_chains(k_ref, vt_ref, qs, kblk, vhead, nk, m_scr, acc_scr):
    tk = vt_ref.shape[3]
    n = len(qs)

    def operands(j):
        off = j * tk if isinstance(j, int) else pl.multiple_of(j * tk, tk)
        k = k_ref[0, pl.ds(off, tk), :]
        vt = vt_ref[0, j]
        ones = jnp.ones((BF16_SUBLANES, tk), BF16)
        vts = {h: jnp.concatenate([vt[h * HEAD_V:(h + 1) * HEAD_V], ones], axis=0) for h in set(vhead)}
        return k, vts

    def scores(k, g):
        return _nt(k[:, kblk[g] * LANES:(kblk[g] + 1) * LANES], qs[g])

    def issue_ahead(k):
        pending = [scores(k, g) for g in range(min(QK_AHEAD, n))]
        for g in range(n):
            s = pending.pop(0)
            if g + QK_AHEAD < n:
                pending.append(scores(k, g + QK_AHEAD))
            yield g, s

    def exact_tile(j):
        k, vts = operands(j)
        for g, s in issue_ahead(k):
            m = m_scr[g]
            mn = jnp.maximum(m, jnp.max(s, axis=0, keepdims=True))
            p = jnp.exp2(s - mn).astype(BF16)
            acc_scr[g] = acc_scr[g] * jnp.exp2(m - mn) + _mm(vts[vhead[g]], p)
            m_scr[g] = mn

    def lazy_tile(j, first, worst):
        k, vts = operands(j)
        for g, s in issue_ahead(k):
            m = m_scr[g]
            top = jnp.max(s, axis=0, keepdims=True)
            p = jnp.exp2(s - m).astype(BF16)
            mn = jnp.maximum(m, top)
            acc_scr[g] = (acc_scr[g] + _mm(vts[vhead[g]], p)) * jnp.exp2(m - mn)
            m_scr[g] = mn
            off = jnp.abs(top - m) if first else top - m
            worst = off if worst is None else jnp.maximum(worst, off)
        return worst

    def reset(m0):
        m_scr[...] = jnp.full(m_scr.shape, m0, F32)
        acc_scr[...] = jnp.zeros(acc_scr.shape, F32)

    reset(0.0)
    worst = lazy_tile(0, True, None)
    worst = lax.fori_loop(1, nk, lambda j, w: lazy_tile(j, False, w), worst)

    @pl.when(jnp.max(worst) > LAZY_MAX_RISE)
    def _():
        reset(NEG)

        def step(j, c):
            exact_tile(j)
            return c

        lax.fori_loop(0, nk, step, 0)

    return [acc_scr[g] for g in range(n)]


def _normalised(acc):
    return acc[:HEAD_V] / acc[HEAD_V:HEAD_V + 1]


def _mla_body(q_ref, k_ref, vt_ref, o_ref, m_scr, acc_scr, *, nk):
    heads = list(range(MLA_HEADS))
    qs = [q_ref[0, :, h * LANES:(h + 1) * LANES] for h in heads]
    accs = _flash_chains(k_ref, vt_ref, qs, heads, heads, nk, m_scr, acc_scr)
    o_ref[0] = jnp.concatenate([_normalised(a) for a in accs], axis=0).T.astype(o_ref.dtype)


def _dense_attn_call(body, name, q, k, vt, extra, out_width, tq, chains):
    B, S, wq = q.shape
    nk, wv, tk = vt.shape[1:]
    return pl.pallas_call(
        body,
        out_shape=jax.ShapeDtypeStruct((B, S, out_width), BF16),
        grid=(B, S // tq),
        in_specs=[pl.BlockSpec((1, tq, wq), lambda b, i: (b, i, 0)),
                  pl.BlockSpec((1, S, wq), lambda b, i: (b, 0, 0)),
                  pl.BlockSpec((1, nk, wv, tk), lambda b, i: (b, 0, 0, 0)),
                  *[pl.BlockSpec(a.shape, lambda b, i: (0, 0)) for a in extra]],
        out_specs=pl.BlockSpec((1, tq, out_width), lambda b, i: (b, i, 0)),
        scratch_shapes=[pltpu.VMEM((chains, 1, tq), F32),
                        pltpu.VMEM((chains, HEAD_V + BF16_SUBLANES, tq), F32)],
        compiler_params=pltpu.CompilerParams(
            dimension_semantics=("parallel", "arbitrary"), vmem_limit_bytes=VMEM_LIMIT),
        name=name,
    )(q, k, vt, *extra)


def _mla_attn(q, k, vt, tq):
    return _dense_attn_call(functools.partial(_mla_body, nk=vt.shape[1]), "mla_attn", q, k, vt, (), MLA_WIDTH, tq,
                            MLA_HEADS)


def _diff_body(q_ref, k_ref, vt_ref, lam_ref, sg_ref, o_ref, m_scr, acc_scr, *, nk, lam_init):
    lane = lax.broadcasted_iota(jnp.int32, (1, LANES), 1)
    per_blk = LANES // DIFF_QK
    chains = list(range(2 * DIFF_HEADS))
    qs = []
    for g in chains:
        qb = q_ref[0, :, (g // per_blk) * LANES:(g // per_blk + 1) * LANES]
        lo = (g % per_blk) * DIFF_QK
        qs.append(jnp.where((lane >= lo) & (lane < lo + DIFF_QK), qb, jnp.zeros_like(qb)))
    accs = _flash_chains(k_ref, vt_ref, qs, [g // per_blk for g in chains], [g // 2 for g in chains], nk,
                         m_scr, acc_scr)

    lv = lam_ref[...]
    lam = (jnp.exp(jnp.sum(lv[0:1] * lv[1:2], axis=-1, keepdims=True))
           - jnp.exp(jnp.sum(lv[2:3] * lv[3:4], axis=-1, keepdims=True)) + lam_init)
    outs = []
    for h in range(DIFF_HEADS):
        o = _normalised(accs[2 * h]) - lam * _normalised(accs[2 * h + 1])
        r = lax.rsqrt(jnp.sum(o * o, axis=0, keepdims=True) * (1.0 / DIFF_V) + EPS)
        outs.append(o * r * (sg_ref[...] * (1.0 - lam_init)))
    o_ref[0] = jnp.concatenate(outs, axis=0).T.astype(o_ref.dtype)


def _diff_attn(q, k, vt, lamv, sgt, lam_init, tq):
    body = functools.partial(_diff_body, nk=vt.shape[1], lam_init=lam_init)
    return _dense_attn_call(body, "diff_attn", q, k, vt, (lamv, sgt), DIFF_WIDTH, tq, 2 * DIFF_HEADS)


def _na_body(q_ref, k_ref, vt_ref, bm_ref, o_ref, *, nrb):
    rb = pl.program_id(1)
    variant = jnp.where(rb == 0, 0, jnp.where(rb == nrb - 1, 2, 1))
    chunk0 = jnp.clip(rb - 1, 0, nrb - 3)
    tq = NA_RB * GRID_W
    off = pl.multiple_of(chunk0 * tq, tq)
    nkeys = NA_KROWS * GRID_W
    k = k_ref[0, pl.ds(off, nkeys), :]
    vt = jnp.concatenate([vt_ref[0, chunk0 + c] for c in range(NA_KROWS // NA_RB)], axis=1)
    ones = jnp.ones((BF16_SUBLANES, nkeys), BF16)
    lane = lax.broadcasted_iota(jnp.int32, (1, LANES), 1)
    per_blk = LANES // NA_DIM

    def scores(h):
        blk = slice((h // per_blk) * LANES, (h // per_blk + 1) * LANES)
        qb = q_ref[0, :, blk]
        lo = (h % per_blk) * NA_DIM
        qh = jnp.where((lane >= lo) & (lane < lo + NA_DIM), qb, jnp.zeros_like(qb))
        return _nt(k[:, blk], qh) + bm_ref[h, variant]

    def attend(shifted):
        pending = [scores(h) for h in range(min(QK_AHEAD, NA_HEADS))]
        outs, worst = [], None
        for h in range(NA_HEADS):
            s = pending.pop(0)
            if h + QK_AHEAD < NA_HEADS:
                pending.append(scores(h + QK_AHEAD))
            top = jnp.max(s, axis=0, keepdims=True)
            p = jnp.exp2(s - top if shifted else s).astype(BF16)
            vt_ext = jnp.concatenate([vt[h * NA_DIM:(h + 1) * NA_DIM], ones], axis=0)
            acc = _mm(vt_ext, p)
            outs.append(acc[:NA_DIM] / acc[NA_DIM:NA_DIM + 1])
            worst = jnp.abs(top) if worst is None else jnp.maximum(worst, jnp.abs(top))
        return jnp.concatenate(outs, axis=0).T.astype(o_ref.dtype), worst

    out, worst = attend(shifted=False)
    o_ref[0] = out

    @pl.when(jnp.max(worst) > LAZY_MAX_RISE)
    def _():
        o_ref[0] = attend(shifted=True)[0]


def _na_attn(q, k, vt, bm):
    B, S, _ = q.shape
    tq = NA_RB * GRID_W
    nrb = S // tq
    return pl.pallas_call(
        functools.partial(_na_body, nrb=nrb),
        out_shape=jax.ShapeDtypeStruct((B, S, NA_WIDTH), BF16),
        grid=(B, nrb),
        in_specs=[pl.BlockSpec((1, tq, NA_WIDTH), lambda b, i: (b, i, 0)),
                  pl.BlockSpec((1, S, NA_WIDTH), lambda b, i: (b, 0, 0)),
                  pl.BlockSpec((1, nrb, NA_WIDTH, tq), lambda b, i: (b, 0, 0, 0)),
                  _resident(bm)],
        out_specs=pl.BlockSpec((1, tq, NA_WIDTH), lambda b, i: (b, i, 0)),
        compiler_params=pltpu.CompilerParams(
            dimension_semantics=("parallel", "arbitrary"), vmem_limit_bytes=VMEM_LIMIT),
        name="na_attn",
    )(q, k, vt, bm)


def _na_bias_table(rpb, rows):
    nrb = rows // NA_RB
    W = GRID_W
    H = rpb.shape[0]
    pad_c = W - NA_KC
    rp = jnp.pad(jnp.flip(rpb.astype(F32), axis=-1), ((0, 0), (0, 0), (pad_c, pad_c + 1)))
    skew = jnp.tile(rp, (1, 1, W))[:, :, :W * (2 * W - 1)].reshape(H, -1, W, 2 * W - 1)
    pad_lo, pad_hi = NA_KR // 2, NA_KROWS - NA_KR
    t = jnp.pad(skew[:, :, :, W - 1:] * LOG2E, ((0, 0), (pad_lo, pad_hi), (0, 0), (0, 0)))
    kri = np.arange(NA_KROWS)[:, None, None]
    kc = np.arange(W)[None, :, None]
    c = np.arange(W)[None, None, :]
    c0 = np.clip(c - NA_KC // 2, 0, W - NA_KC)
    col_ok = (kc >= c0) & (kc < c0 + NA_KC)
    variants = []
    for rb in (0, 1, nrb - 1):
        start = NA_RB * min(max(rb - 1, 0), nrb - 3)
        per_row = []
        for ri in range(NA_RB):
            r = NA_RB * rb + ri
            r0 = min(max(r - NA_KR // 2, 0), rows - NA_KR)
            d0 = start - r + (NA_KR - 1) + pad_lo
            assert 0 <= d0 and d0 + NA_KROWS <= t.shape[1]
            ok = ((start + kri >= r0) & (start + kri < r0 + NA_KR)) & col_ok
            per_row.append(jnp.where(ok[None], t[:, d0:d0 + NA_KROWS], NEG))
        variants.append(jnp.concatenate(per_row, axis=-1))
    return jnp.stack(variants, axis=1).reshape(H, 3, NA_KROWS * W, NA_RB * W)


def _outproj_body(x_ref, om_ref, on_ref, od_ref, gt_ref, gate_ref, w_ref, o_ref):
    def branch(o_ref_, a, b):
        mix = (o_ref_[0].astype(F32) * gt_ref[0, :, a:b].astype(F32)).astype(BF16)
        return _mm(mix, w_ref[a:b, :])

    y = (branch(om_ref, 0, MLA_WIDTH) + branch(on_ref, MLA_WIDTH, MLA_WIDTH + NA_WIDTH)
         + branch(od_ref, MLA_WIDTH + NA_WIDTH, D_MODEL))
    o_ref[0] = x_ref[0] + gate_ref[0, 2:3, :] * y


def _outproj(x, om, on, od, gt, mod_b, w_out, tm):
    B, S, _ = x.shape
    tok = lambda w: pl.BlockSpec((1, tm, w), lambda b, i: (b, i, 0))
    return pl.pallas_call(
        _outproj_body,
        out_shape=jax.ShapeDtypeStruct((B, S, D_MODEL), F32),
        grid=(B, S // tm),
        in_specs=[tok(D_MODEL), tok(MLA_WIDTH), tok(NA_WIDTH), tok(DIFF_WIDTH), tok(D_MODEL),
                  pl.BlockSpec((1, 3, D_MODEL), lambda b, i: (b, 0, 0)),
                  _resident(w_out)],
        out_specs=tok(D_MODEL),
        compiler_params=pltpu.CompilerParams(
            dimension_semantics=("parallel", "parallel"), vmem_limit_bytes=VMEM_LIMIT),
        name="outproj",
    )(x, om, on, od, gt, mod_b, w_out)


def _swap_halves(g, half):
    return jnp.concatenate([g[..., half:], g[..., :half]], axis=-1)


def _pad_cols(w, left, total):
    return jnp.pad(w, ((0, 0), (left, total - left - w.shape[-1])))


def _prep_layer(w_in, w_uq, w_ukv, norm_g, q_lat_g, kv_lat_g, mla_q_g, mla_k_g,
                na_q_g, na_k_g, diff_q_g, diff_k_g):
    sizes = (Q_LORA, KV_LORA, MLA_ROPE, 3 * NA_WIDTH, 3 * DIFF_WIDTH, D_MODEL)
    splits = np.cumsum(sizes)[:-1].tolist()
    w_cq, w_ckv, w_kpe, w_na, w_diff, w_gate = jnp.split(w_in, splits, axis=-1)
    hr = MLA_ROPE // 2
    hd = DIFF_QK // 2
    w_qd, w_kd, w_vd = jnp.split(w_diff, 3, axis=-1)
    w_nq, w_nk, w_nv = jnp.split(w_na, 3, axis=-1)
    w_ext = jnp.concatenate([
        w_cq, w_ckv,
        _pad_cols(w_kpe, MLA_NOPE, LANES), w_nq, w_nk, w_qd, w_kd, w_gate], axis=-1).astype(BF16)
    wnvt = w_nv.T.astype(BF16)
    wdvt = w_vd.T.astype(BF16)

    uq = w_uq.reshape(Q_LORA, MLA_HEADS, MLA_QK)
    wuq = jnp.pad(uq, ((0, 0), (0, 0), (0, LANES - MLA_QK))).reshape(Q_LORA, -1).astype(BF16)

    ukv = w_ukv.reshape(KV_LORA, MLA_HEADS, MLA_NOPE + MLA_V)
    wuk = jnp.pad(ukv[..., :MLA_NOPE], ((0, 0), (0, 0), (0, LANES - MLA_NOPE))).reshape(KV_LORA, -1).astype(BF16)
    wuvt = ukv[..., MLA_NOPE:].reshape(KV_LORA, -1).T.astype(BF16)

    def row(v):
        return jnp.pad(v.astype(F32), (0, D_MODEL - v.shape[0]))

    def mla_rows(g):
        rot = jnp.pad(_swap_halves(g[MLA_NOPE:], hr), (MLA_NOPE, 0))
        return row(g), row(rot)

    def diff_rows(g):
        n = DIFF_WIDTH // DIFF_QK
        return row(jnp.tile(g, n)), row(jnp.tile(_swap_halves(g, hd), n))

    rows = [row(norm_g), row(q_lat_g), row(kv_lat_g), *mla_rows(mla_q_g), *mla_rows(mla_k_g),
            row(jnp.tile(na_q_g, NA_HEADS)), row(jnp.tile(na_k_g, NA_HEADS)),
            *diff_rows(diff_q_g), *diff_rows(diff_k_g)]
    rows += [jnp.zeros((D_MODEL,), F32)] * (G_ROWS - len(rows))
    return w_ext, wuq, wuk, wuvt, wnvt, wdvt, jnp.stack(rows)


def _rope_tables(S):
    def cs(dim):
        inv = ROPE_THETA ** (-jnp.arange(0, dim, 2, dtype=F32) / dim)
        ang = jnp.arange(S, dtype=F32)[:, None] * inv[None, :]
        return jnp.cos(ang), jnp.sin(ang)

    c, s = cs(MLA_ROPE)
    pad = LANES - MLA_QK
    cm = jnp.concatenate([jnp.ones((S, MLA_NOPE), F32), c, c, jnp.zeros((S, pad), F32)], axis=-1)
    sm = jnp.concatenate([jnp.zeros((S, MLA_NOPE), F32), s, s, jnp.zeros((S, pad), F32)], axis=-1)
    c, s = cs(DIFF_QK)
    n = LANES // DIFF_QK
    cd = jnp.tile(jnp.concatenate([c, c], axis=-1), (1, n))
    sd = jnp.tile(jnp.concatenate([s, s], axis=-1), (1, n))
    return cm, sm, cd, sd


def _tile(S, pref):
    t = pref
    while S % t:
        t //= 2
    return t


def kernel(x, c, ada_w, ada_b, norm_g, w_in, q_lat_g, w_uq, kv_lat_g, w_ukv, mla_q_g, mla_k_g,
           na_q_g, na_k_g, na_rpb, diff_q_g, diff_k_g, lam_q1, lam_k1, lam_q2, lam_k2, subln_g, w_out):
    B, S, D = x.shape
    L = ada_w.shape[0]
    rows = S // GRID_W
    assert D == D_MODEL and S % (NA_RB * GRID_W) == 0 and rows >= NA_KROWS

    tabs = _rope_tables(S)
    mod = _adaln(c, ada_w, ada_b)
    tm = _tile(S, TILE_M)
    tk = _tile(S, TILE_K)
    tqd = _tile(S, TILE_Q_DIFF)

    weights = jax.vmap(_prep_layer)(w_in, w_uq, w_ukv, norm_g, q_lat_g, kv_lat_g, mla_q_g, mla_k_g,
                                    na_q_g, na_k_g, diff_q_g, diff_k_g)
    bias = jax.vmap(functools.partial(_na_bias_table, rows=rows))(na_rpb)
    mod_b = mod.transpose(0, 2, 1, 3)
    lamv = jnp.stack([lam_q1, lam_k1, lam_q2, lam_k2], axis=1).astype(F32)
    sgt = jnp.broadcast_to(subln_g.astype(F32)[:, :, None], (L, DIFF_V, tqd))
    w_out_b = w_out.astype(BF16)

    h = x
    for i in range(L):
        lam_init = 0.8 - 0.6 * math.exp(-0.3 * i)
        w_ext, wuq, wuk, wuvt, wnvt, wdvt, gv = (w[i] for w in weights)
        qm, km, vmt, nq, nk, nvt, dq, dk, dvt, gt = _inproj(h, mod_b[i], gv, w_ext, wuq, wuk, wuvt, wnvt, wdvt,
                                                            tabs, tm, tk)
        om = _mla_attn(qm, km, vmt, _tile(S, TILE_Q_MLA))
        on = _na_attn(nq, nk, nvt, bias[i])
        od = _diff_attn(dq, dk, dvt, lamv[i], sgt[i], lam_init, tqd)
        h = _outproj(h, om, on, od, gt, mod_b[i], w_out_b[i], tm)
    return h
```

```python
import functools
import math

import numpy as np
import jax
import jax.numpy as jnp
from jax import lax
from jax.experimental import pallas as pl
from jax.experimental.pallas import tpu as pltpu

F32 = jnp.float32
BF16 = jnp.bfloat16

D_MODEL = 1024
GRID_W = 64
EPS = 1e-6
ROPE_THETA = 10000.0
MLA_HEADS, MLA_NOPE, MLA_ROPE, MLA_V = 6, 64, 32, 64
MLA_QK = MLA_NOPE + MLA_ROPE
Q_LORA, KV_LORA = 256, 128
NA_HEADS, NA_DIM, NA_KR, NA_KC = 6, 64, 8, 16
DIFF_HEADS, DIFF_QK = 4, 32
DIFF_V = 2 * DIFF_QK
MLA_WIDTH, NA_WIDTH, DIFF_WIDTH = MLA_HEADS * MLA_V, NA_HEADS * NA_DIM, DIFF_HEADS * DIFF_V
HEAD_V = 64

LANES = 128
BF16_SUBLANES = 16
MXU_N = 256
VMEM_LIMIT = 52 * 1024 * 1024

C_CQ, C_CKV, C_KPE = 0, 256, 384
C_NA = 512
C_DIFF = 1280
C_GATE = 1792
N_EXT = 2816

G_NORM, G_QLAT, G_KVLAT, G_MQ, G_MQR, G_MK, G_MKR, G_NQ, G_NK, G_DQ, G_DQR, G_DK, G_DKR = range(13)
G_ROWS = 16

TILE_M = 1024
TILE_Q_MLA = 1024
TILE_Q_DIFF = 1024
TILE_K = 512
QK_AHEAD = 3

NA_RB = 4
NA_STEP_BLOCKS = 2
NA_KROWS = 12
NEG = -1e30
LOG2E = math.log2(math.e)
LAZY_MAX_RISE = 64.0


def _nt(a, b):
    return lax.dot_general(a, b, (((1,), (1,)), ((), ())), preferred_element_type=F32)


def _mm(a, b):
    return jnp.dot(a, b, preferred_element_type=F32)


def _seg_ones(seg, width):
    sh = int(math.log2(seg))
    r = lax.broadcasted_iota(jnp.int32, (width, width), 0) >> sh
    c = lax.broadcasted_iota(jnp.int32, (width, width), 1) >> sh
    return jnp.where(r == c, 1.0, 0.0).astype(BF16)


def _seg_rms_scale(x, seg):
    x2 = (x * x).astype(BF16)
    width = x.shape[1]
    parts = []
    for a in range(0, width, MXU_N):
        w = min(MXU_N, width - a)
        parts.append(_mm(x2[:, a:a + w], _seg_ones(seg, w)))
    ss = parts[0] if len(parts) == 1 else jnp.concatenate(parts, axis=1)
    return lax.rsqrt(ss * (1.0 / seg) + EPS)


def _rotate_half(x, group):
    half = group // 2
    lane = lax.broadcasted_iota(jnp.int32, (1, LANES), 1)
    first = (lane & (group - 1)) < half
    from_above = pltpu.roll(x, LANES - half, 1)
    from_below = pltpu.roll(x, half, 1)
    return jnp.where(first, -from_above, from_below)


def _resident(a):
    zeros = (0,) * a.ndim
    return pl.BlockSpec(a.shape, lambda *_: zeros, pipeline_mode=pl.Buffered(1))


def _adaln_body(c_ref, w_ref, b_ref, o_ref):
    c = c_ref[...]
    a = c / (1.0 + jnp.exp(-c))
    o_ref[0, 0] = jnp.dot(a, w_ref[0], preferred_element_type=F32,
                          precision=lax.Precision.HIGHEST) + b_ref[0, 0]


def _adaln(c, ada_w, ada_b):
    L = ada_w.shape[0]
    B = c.shape[0]
    b4 = ada_b.reshape(L, 3, 1, D_MODEL)
    return pl.pallas_call(
        _adaln_body,
        out_shape=jax.ShapeDtypeStruct((L, 3, B, D_MODEL), F32),
        grid=(L, 3),
        in_specs=[
            pl.BlockSpec((B, D_MODEL), lambda l, j: (0, 0)),
            pl.BlockSpec((1, D_MODEL, D_MODEL), lambda l, j: (l, 0, j)),
            pl.BlockSpec((1, 1, 1, D_MODEL), lambda l, j: (l, j, 0, 0)),
        ],
        out_specs=pl.BlockSpec((1, 1, B, D_MODEL), lambda l, j: (l, j, 0, 0)),
        compiler_params=pltpu.CompilerParams(
            dimension_semantics=("arbitrary", "arbitrary"), vmem_limit_bytes=VMEM_LIMIT),
        name="adaln",
    )(c, ada_w, b4)


def _inproj_body(x_ref, mod_ref, gv_ref, w_ref, wuq_ref, wuk_ref, wuvt_ref, wnvt_ref, wdvt_ref,
                 cm_ref, sm_ref, cd_ref, sd_ref,
                 qm_ref, km_ref, vmt_ref, nq_ref, nk_ref, nvt_ref, dq_ref, dk_ref, dvt_ref, gt_ref):
    def gain(row, width, off=0):
        return gv_ref[row:row + 1, off:off + width]

    x = x_ref[0]
    ms = jnp.mean(x * x, axis=-1, keepdims=True)
    xn = x * lax.rsqrt(ms + EPS) * gain(G_NORM, D_MODEL)
    h = (xn * (1.0 + mod_ref[0, 1:2, :]) + mod_ref[0, 0:1, :]).astype(BF16)

    def proj(a, b):
        return _mm(h, w_ref[:, a:b])

    def lat_norm(v, row, width):
        r = lax.rsqrt(jnp.mean(v * v, axis=-1, keepdims=True) + EPS)
        return (v * r * gain(row, width)).astype(BF16)

    cm, sm = cm_ref[...], sm_ref[...]
    cd, sd = cd_ref[...], sd_ref[...]

    lat = proj(C_CQ, C_NA)
    cqn = lat_norm(lat[:, C_CQ:C_CQ + Q_LORA], G_QLAT, Q_LORA)
    qq = _mm(cqn, wuq_ref[...])
    wq_c = gain(G_MQ, LANES) * cm
    wq_s = gain(G_MQR, LANES) * sm
    q_scale = MLA_QK ** -0.5 * LOG2E
    for hd in range(MLA_HEADS):
        a = qq[:, hd * LANES:(hd + 1) * LANES]
        r = lax.rsqrt(jnp.sum(a * a, axis=-1, keepdims=True) * (1.0 / MLA_QK) + EPS)
        rotated = _rotate_half(a, MLA_ROPE)
        qm_ref[0, :, hd * LANES:(hd + 1) * LANES] = ((r * q_scale) * (a * wq_c + rotated * wq_s)).astype(BF16)

    ckvn = lat_norm(lat[:, C_CKV:C_CKV + KV_LORA], G_KVLAT, KV_LORA)
    _store_chunks(vmt_ref, _nt(wuvt_ref[...], ckvn).astype(BF16))
    kn = _mm(ckvn, wuk_ref[...])
    kpe = lat[:, C_KPE:C_KPE + LANES]
    wk_c = gain(G_MK, LANES) * cm
    k_rot = _rotate_half(kpe, MLA_ROPE) * (gain(G_MKR, LANES) * sm)
    for hd in range(MLA_HEADS):
        a = kn[:, hd * LANES:(hd + 1) * LANES] + kpe
        r = lax.rsqrt(jnp.sum(a * a, axis=-1, keepdims=True) * (1.0 / MLA_QK) + EPS)
        km_ref[0, :, hd * LANES:(hd + 1) * LANES] = (r * (a * wk_c + k_rot)).astype(BF16)

    nqk = proj(C_NA, C_NA + 2 * NA_WIDTH)
    na_scale = NA_DIM ** -0.5 * LOG2E
    a = nqk[:, :NA_WIDTH]
    nq_ref[0] = (a * (_seg_rms_scale(a, NA_DIM) * na_scale) * gain(G_NQ, NA_WIDTH)).astype(BF16)
    a = nqk[:, NA_WIDTH:]
    nk_ref[0] = (a * _seg_rms_scale(a, NA_DIM) * gain(G_NK, NA_WIDTH)).astype(BF16)
    _store_chunks(nvt_ref, _nt(wnvt_ref[...], h).astype(BF16))

    dd = proj(C_DIFF, C_DIFF + 2 * DIFF_WIDTH)
    d_scale = DIFF_QK ** -0.5 * LOG2E
    reps = DIFF_WIDTH // LANES
    cdw, sdw = jnp.concatenate([cd] * reps, axis=1), jnp.concatenate([sd] * reps, axis=1)
    for (o_ref, base, grow, grot, sc) in ((dq_ref, 0, G_DQ, G_DQR, d_scale), (dk_ref, DIFF_WIDTH, G_DK, G_DKR, 1.0)):
        a = dd[:, base:base + DIFF_WIDTH]
        b = jnp.concatenate([_rotate_half(a[:, i * LANES:(i + 1) * LANES], DIFF_QK) for i in range(reps)], axis=1)
        r = _seg_rms_scale(a, DIFF_QK)
        o_ref[0] = ((r * sc) * (a * (gain(grow, DIFF_WIDTH) * cdw) + b * (gain(grot, DIFF_WIDTH) * sdw))).astype(BF16)
    _store_chunks(dvt_ref, _nt(wdvt_ref[...], h).astype(BF16))

    g = proj(C_GATE, N_EXT)
    gt_ref[0] = (g / (1.0 + jnp.exp(-g))).astype(BF16)


def _store_chunks(ref, val):
    tk = ref.shape[3]
    for c in range(ref.shape[1]):
        ref[0, c] = val[:, c * tk:(c + 1) * tk]


def _inproj(x, mod_b, gv, w_ext, wuq, wuk, wuvt, wnvt, wdvt, tabs, tm, tk):
    B, S, _ = x.shape
    nt = S // tm
    cm, sm, cd, sd = tabs
    tok = lambda w: pl.BlockSpec((1, tm, w), lambda b, i: (b, i, 0))
    def tr(w, c):
        if c <= tm:
            return pl.BlockSpec((1, tm // c, w, c), lambda b, i: (b, i, 0, 0))
        per = c // tm
        return pl.BlockSpec((1, 1, w, tm), lambda b, i: (b, i // per, 0, i % per))
    tab = pl.BlockSpec((tm, LANES), lambda b, i: (i, 0))
    tok_out = lambda w: jax.ShapeDtypeStruct((B, S, w), BF16)
    tr_out = lambda w, c: jax.ShapeDtypeStruct((B, S // c, w, c), BF16)
    tna = NA_RB * GRID_W
    weights = (gv, w_ext, wuq, wuk, wuvt, wnvt, wdvt)
    return pl.pallas_call(
        _inproj_body,
        out_shape=[tok_out(MLA_HEADS * LANES), tok_out(MLA_HEADS * LANES), tr_out(MLA_WIDTH, tk),
                   tok_out(NA_WIDTH), tok_out(NA_WIDTH), tr_out(NA_WIDTH, tna),
                   tok_out(DIFF_WIDTH), tok_out(DIFF_WIDTH), tr_out(DIFF_WIDTH, tk), tok_out(D_MODEL)],
        grid=(B, nt),
        in_specs=[tok(D_MODEL), pl.BlockSpec((1, 3, D_MODEL), lambda b, i: (b, 0, 0)),
                  *[_resident(a) for a in weights], tab, tab, tab, tab],
        out_specs=[tok(MLA_HEADS * LANES), tok(MLA_HEADS * LANES), tr(MLA_WIDTH, tk),
                   tok(NA_WIDTH), tok(NA_WIDTH), tr(NA_WIDTH, tna),
                   tok(DIFF_WIDTH), tok(DIFF_WIDTH), tr(DIFF_WIDTH, tk), tok(D_MODEL)],
        compiler_params=pltpu.CompilerParams(
            dimension_semantics=("parallel", "parallel"), vmem_limit_bytes=VMEM_LIMIT),
        name="inproj",
    )(x, mod_b, *weights, cm, sm, cd, sd)


def _flash_chains(k_ref, vt_ref, qs, kblk, vhead, nk, m_scr, acc_scr):
    tk = vt_ref.shape[3]
    n = len(qs)

    def operands(j):
        off = j * tk if isinstance(j, int) else pl.multiple_of(j * tk, tk)
        k = k_ref[0, pl.ds(off, tk), :]
        vt = vt_ref[0, j]
        ones = jnp.ones((BF16_SUBLANES, tk), BF16)
        vts = {h: jnp.concatenate([vt[h * HEAD_V:(h + 1) * HEAD_V], ones], axis=0) for h in set(vhead)}
        return k, vts

    def scores(k, g):
        return _nt(k[:, kblk[g] * LANES:(kblk[g] + 1) * LANES], qs[g])

    def issue_ahead(k):
        pending = [scores(k, g) for g in range(min(QK_AHEAD, n))]
        for g in range(n):
            s = pending.pop(0)
            if g + QK_AHEAD < n:
                pending.append(scores(k, g + QK_AHEAD))
            yield g, s

    def exact_tile(j):
        k, vts = operands(j)
        for g, s in issue_ahead(k):
            m = m_scr[g]
            mn = jnp.maximum(m, jnp.max(s, axis=0, keepdims=True))
            p = jnp.exp2(s - mn).astype(BF16)
            acc_scr[g] = acc_scr[g] * jnp.exp2(m - mn) + _mm(vts[vhead[g]], p)
            m_scr[g] = mn

    def lazy_tile(j, first, worst):
        k, vts = operands(j)
        for g, s in issue_ahead(k):
            m = m_scr[g]
            top = jnp.max(s, axis=0, keepdims=True)
            p = jnp.exp2(s - m).astype(BF16)
            mn = jnp.maximum(m, top)
            acc_scr[g] = (acc_scr[g] + _mm(vts[vhead[g]], p)) * jnp.exp2(m - mn)
            m_scr[g] = mn
            off = jnp.abs(top - m) if first else top - m
            worst = off if worst is None else jnp.maximum(worst, off)
        return worst

    def reset(m0):
        m_scr[...] = jnp.full(m_scr.shape, m0, F32)
        acc_scr[...] = jnp.zeros(acc_scr.shape, F32)

    reset(0.0)
    worst = lazy_tile(0, True, None)
    worst = lax.fori_loop(1, nk, lambda j, w: lazy_tile(j, False, w), worst)

    @pl.when(jnp.max(worst) > LAZY_MAX_RISE)
    def _():
        reset(NEG)

        def step(j, c):
            exact_tile(j)
            return c

        lax.fori_loop(0, nk, step, 0)

    return [acc_scr[g] for g in range(n)]


def _normalised(acc):
    return acc[:HEAD_V] / acc[HEAD_V:HEAD_V + 1]


def _mla_body(q_ref, k_ref, vt_ref, o_ref, m_scr, acc_scr, *, nk):
    heads = list(range(MLA_HEADS))
    qs = [q_ref[0, :, h * LANES:(h + 1) * LANES] for h in heads]
    accs = _flash_chains(k_ref, vt_ref, qs, heads, heads, nk, m_scr, acc_scr)
    o_ref[0] = jnp.concatenate([_normalised(a) for a in accs], axis=0).T.astype(o_ref.dtype)


def _dense_attn_call(body, name, q, k, vt, extra, out_width, tq, chains):
    B, S, wq = q.shape
    nk, wv, tk = vt.shape[1:]
    return pl.pallas_call(
        body,
        out_shape=jax.ShapeDtypeStruct((B, S, out_width), BF16),
        grid=(B, S // tq),
        in_specs=[pl.BlockSpec((1, tq, wq), lambda b, i: (b, i, 0)),
                  pl.BlockSpec((1, S, wq), lambda b, i: (b, 0, 0)),
                  pl.BlockSpec((1, nk, wv, tk), lambda b, i: (b, 0, 0, 0)),
                  *[pl.BlockSpec(a.shape, lambda b, i: (0, 0)) for a in extra]],
        out_specs=pl.BlockSpec((1, tq, out_width), lambda b, i: (b, i, 0)),
        scratch_shapes=[pltpu.VMEM((chains, 1, tq), F32),
                        pltpu.VMEM((chains, HEAD_V + BF16_SUBLANES, tq), F32)],
        compiler_params=pltpu.CompilerParams(
            dimension_semantics=("parallel", "arbitrary"), vmem_limit_bytes=VMEM_LIMIT),
        name=name,
    )(q, k, vt, *extra)


def _mla_attn(q, k, vt, tq):
    return _dense_attn_call(functools.partial(_mla_body, nk=vt.shape[1]), "mla_attn", q, k, vt, (), MLA_WIDTH, tq,
                            MLA_HEADS)


def _diff_body(q_ref, k_ref, vt_ref, lam_ref, sg_ref, o_ref, m_scr, acc_scr, *, nk, lam_init):
    lane = lax.broadcasted_iota(jnp.int32, (1, LANES), 1)
    per_blk = LANES // DIFF_QK
    chains = list(range(2 * DIFF_HEADS))
    qs = []
    for g in chains:
        qb = q_ref[0, :, (g // per_blk) * LANES:(g // per_blk + 1) * LANES]
        lo = (g % per_blk) * DIFF_QK
        qs.append(jnp.where((lane >= lo) & (lane < lo + DIFF_QK), qb, jnp.zeros_like(qb)))
    accs = _flash_chains(k_ref, vt_ref, qs, [g // per_blk for g in chains], [g // 2 for g in chains], nk,
                         m_scr, acc_scr)

    lv = lam_ref[...]
    lam = (jnp.exp(jnp.sum(lv[0:1] * lv[1:2], axis=-1, keepdims=True))
           - jnp.exp(jnp.sum(lv[2:3] * lv[3:4], axis=-1, keepdims=True)) + lam_init)
    outs = []
    for h in range(DIFF_HEADS):
        o = _normalised(accs[2 * h]) - lam * _normalised(accs[2 * h + 1])
        r = lax.rsqrt(jnp.sum(o * o, axis=0, keepdims=True) * (1.0 / DIFF_V) + EPS)
        outs.append(o * r * (sg_ref[...] * (1.0 - lam_init)))
    o_ref[0] = jnp.concatenate(outs, axis=0).T.astype(o_ref.dtype)


def _diff_attn(q, k, vt, lamv, sgt, lam_init, tq):
    body = functools.partial(_diff_body, nk=vt.shape[1], lam_init=lam_init)
    return _dense_attn_call(body, "diff_attn", q, k, vt, (lamv, sgt), DIFF_WIDTH, tq, 2 * DIFF_HEADS)


def _na_body(q_ref, k_ref, vt_ref, bm_ref, o_ref, *, nrb):
    tq = NA_RB * GRID_W
    nkeys = NA_KROWS * GRID_W
    ones = jnp.ones((BF16_SUBLANES, nkeys), BF16)
    lane = lax.broadcasted_iota(jnp.int32, (1, LANES), 1)
    per_blk = LANES // NA_DIM

    def attend(b, shifted):
        rb = pl.program_id(1) * NA_STEP_BLOCKS + b
        variant = jnp.where(rb == 0, 0, jnp.where(rb == nrb - 1, 2, 1))
        chunk0 = jnp.clip(rb - 1, 0, nrb - 3)
        k = k_ref[0, pl.ds(pl.multiple_of(chunk0 * tq, tq), nkeys), :]
        vt = jnp.concatenate([vt_ref[0, chunk0 + c] for c in range(NA_KROWS // NA_RB)], axis=1)

        def scores(h):
            blk = slice((h // per_blk) * LANES, (h // per_blk + 1) * LANES)
            qb = q_ref[0, b * tq:(b + 1) * tq, blk]
            lo = (h % per_blk) * NA_DIM
            qh = jnp.where((lane >= lo) & (lane < lo + NA_DIM), qb, jnp.zeros_like(qb))
            return _nt(k[:, blk], qh) + bm_ref[h, variant]

        pending = [scores(h) for h in range(min(QK_AHEAD, NA_HEADS))]
        outs, worst = [], None
        for h in range(NA_HEADS):
            s = pending.pop(0)
            if h + QK_AHEAD < NA_HEADS:
                pending.append(scores(h + QK_AHEAD))
            top = jnp.max(s, axis=0, keepdims=True)
            p = jnp.exp2(s - top if shifted else s).astype(BF16)
            vt_ext = jnp.concatenate([vt[h * NA_DIM:(h + 1) * NA_DIM], ones], axis=0)
            acc = _mm(vt_ext, p)
            outs.append(acc[:NA_DIM] / acc[NA_DIM:NA_DIM + 1])
            worst = jnp.abs(top) if worst is None else jnp.maximum(worst, jnp.abs(top))
        o_ref[0, b * tq:(b + 1) * tq] = jnp.concatenate(outs, axis=0).T.astype(o_ref.dtype)
        return worst

    def all_blocks(shifted):
        return functools.reduce(jnp.maximum, [attend(b, shifted) for b in range(NA_STEP_BLOCKS)])

    worst = all_blocks(shifted=False)

    @pl.when(jnp.max(worst) > LAZY_MAX_RISE)
    def _():
        all_blocks(shifted=True)


def _na_attn(q, k, vt, bm):
    B, S, _ = q.shape
    tq = NA_RB * GRID_W
    nrb = S // tq
    step = NA_STEP_BLOCKS * tq
    return pl.pallas_call(
        functools.partial(_na_body, nrb=nrb),
        out_shape=jax.ShapeDtypeStruct((B, S, NA_WIDTH), BF16),
        grid=(B, S // step),
        in_specs=[pl.BlockSpec((1, step, NA_WIDTH), lambda b, i: (b, i, 0)),
                  pl.BlockSpec((1, S, NA_WIDTH), lambda b, i: (b, 0, 0)),
                  pl.BlockSpec((1, nrb, NA_WIDTH, tq), lambda b, i: (b, 0, 0, 0)),
                  _resident(bm)],
        out_specs=pl.BlockSpec((1, step, NA_WIDTH), lambda b, i: (b, i, 0)),
        compiler_params=pltpu.CompilerParams(
            dimension_semantics=("parallel", "arbitrary"), vmem_limit_bytes=VMEM_LIMIT),
        name="na_attn",
    )(q, k, vt, bm)


def _na_bias_table(rpb, rows):
    nrb = rows // NA_RB
    W = GRID_W
    H = rpb.shape[0]
    pad_c = W - NA_KC
    rp = jnp.pad(jnp.flip(rpb.astype(F32), axis=-1), ((0, 0), (0, 0), (pad_c, pad_c + 1)))
    skew = jnp.tile(rp, (1, 1, W))[:, :, :W * (2 * W - 1)].reshape(H, -1, W, 2 * W - 1)
    pad_lo, pad_hi = NA_KR // 2, NA_KROWS - NA_KR
    t = jnp.pad(skew[:, :, :, W - 1:] * LOG2E, ((0, 0), (pad_lo, pad_hi), (0, 0), (0, 0)))
    kri = np.arange(NA_KROWS)[:, None, None]
    kc = np.arange(W)[None, :, None]
    c = np.arange(W)[None, None, :]
    c0 = np.clip(c - NA_KC // 2, 0, W - NA_KC)
    col_ok = (kc >= c0) & (kc < c0 + NA_KC)
    variants = []
    for rb in (0, 1, nrb - 1):
        start = NA_RB * min(max(rb - 1, 0), nrb - 3)
        per_row = []
        for ri in range(NA_RB):
            r = NA_RB * rb + ri
            r0 = min(max(r - NA_KR // 2, 0), rows - NA_KR)
            d0 = start - r + (NA_KR - 1) + pad_lo
            assert 0 <= d0 and d0 + NA_KROWS <= t.shape[1]
            ok = ((start + kri >= r0) & (start + kri < r0 + NA_KR)) & col_ok
            per_row.append(jnp.where(ok[None], t[:, d0:d0 + NA_KROWS], NEG))
        variants.append(jnp.concatenate(per_row, axis=-1))
    return jnp.stack(variants, axis=1).reshape(H, 3, NA_KROWS * W, NA_RB * W)


def _outproj_body(x_ref, om_ref, on_ref, od_ref, gt_ref, gate_ref, w_ref, o_ref):
    def branch(o_ref_, a, b):
        mix = (o_ref_[0].astype(F32) * gt_ref[0, :, a:b].astype(F32)).astype(BF16)
        return _mm(mix, w_ref[a:b, :])

    y = (branch(om_ref, 0, MLA_WIDTH) + branch(on_ref, MLA_WIDTH, MLA_WIDTH + NA_WIDTH)
         + branch(od_ref, MLA_WIDTH + NA_WIDTH, D_MODEL))
    o_ref[0] = x_ref[0] + gate_ref[0, 2:3, :] * y


def _outproj(x, om, on, od, gt, mod_b, w_out, tm):
    B, S, _ = x.shape
    tok = lambda w: pl.BlockSpec((1, tm, w), lambda b, i: (b, i, 0))
    return pl.pallas_call(
        _outproj_body,
        out_shape=jax.ShapeDtypeStruct((B, S, D_MODEL), F32),
        grid=(B, S // tm),
        in_specs=[tok(D_MODEL), tok(MLA_WIDTH), tok(NA_WIDTH), tok(DIFF_WIDTH), tok(D_MODEL),
                  pl.BlockSpec((1, 3, D_MODEL), lambda b, i: (b, 0, 0)),
                  _resident(w_out)],
        out_specs=tok(D_MODEL),
        compiler_params=pltpu.CompilerParams(
            dimension_semantics=("parallel", "parallel"), vmem_limit_bytes=VMEM_LIMIT),
        name="outproj",
    )(x, om, on, od, gt, mod_b, w_out)


def _swap_halves(g, half):
    return jnp.concatenate([g[..., half:], g[..., :half]], axis=-1)


def _pad_cols(w, left, total):
    return jnp.pad(w, ((0, 0), (left, total - left - w.shape[-1])))


def _prep_layer(w_in, w_uq, w_ukv, norm_g, q_lat_g, kv_lat_g, mla_q_g, mla_k_g,
                na_q_g, na_k_g, diff_q_g, diff_k_g):
    sizes = (Q_LORA, KV_LORA, MLA_ROPE, 3 * NA_WIDTH, 3 * DIFF_WIDTH, D_MODEL)
    splits = np.cumsum(sizes)[:-1].tolist()
    w_cq, w_ckv, w_kpe, w_na, w_diff, w_gate = jnp.split(w_in, splits, axis=-1)
    hr = MLA_ROPE // 2
    hd = DIFF_QK // 2
    w_qd, w_kd, w_vd = jnp.split(w_diff, 3, axis=-1)
    w_nq, w_nk, w_nv = jnp.split(w_na, 3, axis=-1)
    w_ext = jnp.concatenate([
        w_cq, w_ckv,
        _pad_cols(w_kpe, MLA_NOPE, LANES), w_nq, w_nk, w_qd, w_kd, w_gate], axis=-1).astype(BF16)
    wnvt = w_nv.T.astype(BF16)
    wdvt = w_vd.T.astype(BF16)

    uq = w_uq.reshape(Q_LORA, MLA_HEADS, MLA_QK)
    wuq = jnp.pad(uq, ((0, 0), (0, 0), (0, LANES - MLA_QK))).reshape(Q_LORA, -1).astype(BF16)

    ukv = w_ukv.reshape(KV_LORA, MLA_HEADS, MLA_NOPE + MLA_V)
    wuk = jnp.pad(ukv[..., :MLA_NOPE], ((0, 0), (0, 0), (0, LANES - MLA_NOPE))).reshape(KV_LORA, -1).astype(BF16)
    wuvt = ukv[..., MLA_NOPE:].reshape(KV_LORA, -1).T.astype(BF16)

    def row(v):
        return jnp.pad(v.astype(F32), (0, D_MODEL - v.shape[0]))

    def mla_rows(g):
        rot = jnp.pad(_swap_halves(g[MLA_NOPE:], hr), (MLA_NOPE, 0))
        return row(g), row(rot)

    def diff_rows(g):
        n = DIFF_WIDTH // DIFF_QK
        return row(jnp.tile(g, n)), row(jnp.tile(_swap_halves(g, hd), n))

    rows = [row(norm_g), row(q_lat_g), row(kv_lat_g), *mla_rows(mla_q_g), *mla_rows(mla_k_g),
            row(jnp.tile(na_q_g, NA_HEADS)), row(jnp.tile(na_k_g, NA_HEADS)),
            *diff_rows(diff_q_g), *diff_rows(diff_k_g)]
    rows += [jnp.zeros((D_MODEL,), F32)] * (G_ROWS - len(rows))
    return w_ext, wuq, wuk, wuvt, wnvt, wdvt, jnp.stack(rows)


def _rope_tables(S):
    def cs(dim):
        inv = ROPE_THETA ** (-jnp.arange(0, dim, 2, dtype=F32) / dim)
        ang = jnp.arange(S, dtype=F32)[:, None] * inv[None, :]
        return jnp.cos(ang), jnp.sin(ang)

    c, s = cs(MLA_ROPE)
    pad = LANES - MLA_QK
    cm = jnp.concatenate([jnp.ones((S, MLA_NOPE), F32), c, c, jnp.zeros((S, pad), F32)], axis=-1)
    sm = jnp.concatenate([jnp.zeros((S, MLA_NOPE), F32), s, s, jnp.zeros((S, pad), F32)], axis=-1)
    c, s = cs(DIFF_QK)
    n = LANES // DIFF_QK
    cd = jnp.tile(jnp.concatenate([c, c], axis=-1), (1, n))
    sd = jnp.tile(jnp.concatenate([s, s], axis=-1), (1, n))
    return cm, sm, cd, sd


def _tile(S, pref):
    t = pref
    while S % t:
        t //= 2
    return t


def kernel(x, c, ada_w, ada_b, norm_g, w_in, q_lat_g, w_uq, kv_lat_g, w_ukv, mla_q_g, mla_k_g,
           na_q_g, na_k_g, na_rpb, diff_q_g, diff_k_g, lam_q1, lam_k1, lam_q2, lam_k2, subln_g, w_out):
    B, S, D = x.shape
    L = ada_w.shape[0]
    rows = S // GRID_W
    assert D == D_MODEL and S % (NA_STEP_BLOCKS * NA_RB * GRID_W) == 0 and rows >= NA_KROWS

    tabs = _rope_tables(S)
    mod = _adaln(c, ada_w, ada_b)
    tm = _tile(S, TILE_M)
    tk = _tile(S, TILE_K)
    tqd = _tile(S, TILE_Q_DIFF)

    weights = jax.vmap(_prep_layer)(w_in, w_uq, w_ukv, norm_g, q_lat_g, kv_lat_g, mla_q_g, mla_k_g,
                                    na_q_g, na_k_g, diff_q_g, diff_k_g)
    bias = jax.vmap(functools.partial(_na_bias_table, rows=rows))(na_rpb)
    mod_b = mod.transpose(0, 2, 1, 3)
    lamv = jnp.stack([lam_q1, lam_k1, lam_q2, lam_k2], axis=1).astype(F32)
    sgt = jnp.broadcast_to(subln_g.astype(F32)[:, :, None], (L, DIFF_V, tqd))
    w_out_b = w_out.astype(BF16)

    h = x
    for i in range(L):
        lam_init = 0.8 - 0.6 * math.exp(-0.3 * i)
        w_ext, wuq, wuk, wuvt, wnvt, wdvt, gv = (w[i] for w in weights)
        qm, km, vmt, nq, nk, nvt, dq, dk, dvt, gt = _inproj(h, mod_b[i], gv, w_ext, wuq, wuk, wuvt, wnvt, wdvt,
                                                            tabs, tm, tk)
        om = _mla_attn(qm, km, vmt, _tile(S, TILE_Q_MLA))
        on = _na_attn(nq, nk, nvt, bias[i])
        od = _diff_attn(dq, dk, dvt, lamv[i], sgt[i], lam_init, tqd)
        h = _outproj(h, om, on, od, gt, mod_b[i], w_out_b[i], tm)
    return h
```

```python
import functools
import math

import numpy as np
import jax
import jax.numpy as jnp
from jax import lax
from jax.experimental import pallas as pl
from jax.experimental.pallas import tpu as pltpu

F32 = jnp.float32
BF16 = jnp.bfloat16

D_MODEL = 1024
GRID_W = 64
EPS = 1e-6
ROPE_THETA = 10000.0
MLA_HEADS, MLA_NOPE, MLA_ROPE, MLA_V = 6, 64, 32, 64
MLA_QK = MLA_NOPE + MLA_ROPE
Q_LORA, KV_LORA = 256, 128
NA_HEADS, NA_DIM, NA_KR, NA_KC = 6, 64, 8, 16
DIFF_HEADS, DIFF_QK = 4, 32
DIFF_V = 2 * DIFF_QK
MLA_WIDTH, NA_WIDTH, DIFF_WIDTH = MLA_HEADS * MLA_V, NA_HEADS * NA_DIM, DIFF_HEADS * DIFF_V
HEAD_V = 64

LANES = 128
BF16_SUBLANES = 16
MXU_N = 256
VMEM_LIMIT = 52 * 1024 * 1024

C_CQ, C_CKV, C_KPE = 0, 256, 384
C_NA = 512
C_DIFF = 1280
C_GATE = 1792
N_EXT = 2816

G_NORM, G_QLAT, G_KVLAT, G_MQ, G_MQR, G_MK, G_MKR, G_NQ, G_NK, G_DQ, G_DQR, G_DK, G_DKR = range(13)
G_ROWS = 16

TILE_M = 1024
TILE_Q_MLA = 1024
TILE_Q_DIFF = 1024
TILE_K = 512
LAZY_UNROLL = 2
QK_AHEAD = 3

NA_RB = 4
NA_STEP_BLOCKS = 4
NA_KROWS = 12
NEG = -1e30
LOG2E = math.log2(math.e)
LAZY_MAX_RISE = 64.0


def _nt(a, b):
    return lax.dot_general(a, b, (((1,), (1,)), ((), ())), preferred_element_type=F32)


def _mm(a, b):
    return jnp.dot(a, b, preferred_element_type=F32)


def _seg_ones(seg, width):
    sh = int(math.log2(seg))
    r = lax.broadcasted_iota(jnp.int32, (width, width), 0) >> sh
    c = lax.broadcasted_iota(jnp.int32, (width, width), 1) >> sh
    return jnp.where(r == c, 1.0, 0.0).astype(BF16)


def _seg_rms_scale(x, seg):
    x2 = (x * x).astype(BF16)
    width = x.shape[1]
    parts = []
    for a in range(0, width, MXU_N):
        w = min(MXU_N, width - a)
        parts.append(_mm(x2[:, a:a + w], _seg_ones(seg, w)))
    ss = parts[0] if len(parts) == 1 else jnp.concatenate(parts, axis=1)
    return lax.rsqrt(ss * (1.0 / seg) + EPS)


def _rotate_half(x, group):
    half = group // 2
    lane = lax.broadcasted_iota(jnp.int32, (1, LANES), 1)
    first = (lane & (group - 1)) < half
    from_above = pltpu.roll(x, LANES - half, 1)
    from_below = pltpu.roll(x, half, 1)
    return jnp.where(first, -from_above, from_below)


def _resident(a):
    zeros = (0,) * a.ndim
    return pl.BlockSpec(a.shape, lambda *_: zeros, pipeline_mode=pl.Buffered(1))


def _adaln_body(c_ref, w_ref, b_ref, o_ref):
    c = c_ref[...]
    a = c / (1.0 + jnp.exp(-c))
    o_ref[0, 0] = jnp.dot(a, w_ref[0], preferred_element_type=F32,
                          precision=lax.Precision.HIGHEST) + b_ref[0, 0]


def _adaln(c, ada_w, ada_b):
    L = ada_w.shape[0]
    B = c.shape[0]
    b4 = ada_b.reshape(L, 3, 1, D_MODEL)
    return pl.pallas_call(
        _adaln_body,
        out_shape=jax.ShapeDtypeStruct((L, 3, B, D_MODEL), F32),
        grid=(L, 3),
        in_specs=[
            pl.BlockSpec((B, D_MODEL), lambda l, j: (0, 0)),
            pl.BlockSpec((1, D_MODEL, D_MODEL), lambda l, j: (l, 0, j)),
            pl.BlockSpec((1, 1, 1, D_MODEL), lambda l, j: (l, j, 0, 0)),
        ],
        out_specs=pl.BlockSpec((1, 1, B, D_MODEL), lambda l, j: (l, j, 0, 0)),
        compiler_params=pltpu.CompilerParams(
            dimension_semantics=("arbitrary", "arbitrary"), vmem_limit_bytes=VMEM_LIMIT),
        name="adaln",
    )(c, ada_w, b4)


def _inproj_body(x_ref, mod_ref, gv_ref, w_ref, wuq_ref, wuk_ref, wuvt_ref, wnvt_ref, wdvt_ref,
                 cm_ref, sm_ref, cd_ref, sd_ref,
                 qm_ref, km_ref, vmt_ref, nq_ref, nk_ref, nvt_ref, dq_ref, dk_ref, dvt_ref, gt_ref):
    def gain(row, width, off=0):
        return gv_ref[row:row + 1, off:off + width]

    x = x_ref[0]
    ms = jnp.mean(x * x, axis=-1, keepdims=True)
    xn = x * lax.rsqrt(ms + EPS) * gain(G_NORM, D_MODEL)
    h = (xn * (1.0 + mod_ref[0, 1:2, :]) + mod_ref[0, 0:1, :]).astype(BF16)

    def proj(a, b):
        return _mm(h, w_ref[:, a:b])

    def lat_norm(v, row, width):
        r = lax.rsqrt(jnp.mean(v * v, axis=-1, keepdims=True) + EPS)
        return (v * r * gain(row, width)).astype(BF16)

    cm, sm = cm_ref[...], sm_ref[...]
    cd, sd = cd_ref[...], sd_ref[...]

    lat = proj(C_CQ, C_NA)
    cqn = lat_norm(lat[:, C_CQ:C_CQ + Q_LORA], G_QLAT, Q_LORA)
    qq = _mm(cqn, wuq_ref[...])
    wq_c = gain(G_MQ, LANES) * cm
    wq_s = gain(G_MQR, LANES) * sm
    q_scale = MLA_QK ** -0.5 * LOG2E
    for hd in range(MLA_HEADS):
        a = qq[:, hd * LANES:(hd + 1) * LANES]
        r = lax.rsqrt(jnp.sum(a * a, axis=-1, keepdims=True) * (1.0 / MLA_QK) + EPS)
        rotated = _rotate_half(a, MLA_ROPE)
        qm_ref[0, :, hd * LANES:(hd + 1) * LANES] = ((r * q_scale) * (a * wq_c + rotated * wq_s)).astype(BF16)

    ckvn = lat_norm(lat[:, C_CKV:C_CKV + KV_LORA], G_KVLAT, KV_LORA)
    _store_chunks(vmt_ref, _nt(wuvt_ref[...], ckvn).astype(BF16))
    kn = _mm(ckvn, wuk_ref[...])
    kpe = lat[:, C_KPE:C_KPE + LANES]
    wk_c = gain(G_MK, LANES) * cm
    k_rot = _rotate_half(kpe, MLA_ROPE) * (gain(G_MKR, LANES) * sm)
    for hd in range(MLA_HEADS):
        a = kn[:, hd * LANES:(hd + 1) * LANES] + kpe
        r = lax.rsqrt(jnp.sum(a * a, axis=-1, keepdims=True) * (1.0 / MLA_QK) + EPS)
        km_ref[0, :, hd * LANES:(hd + 1) * LANES] = (r * (a * wk_c + k_rot)).astype(BF16)

    nqk = proj(C_NA, C_NA + 2 * NA_WIDTH)
    na_scale = NA_DIM ** -0.5 * LOG2E
    a = nqk[:, :NA_WIDTH]
    nq_ref[0] = (a * (_seg_rms_scale(a, NA_DIM) * na_scale) * gain(G_NQ, NA_WIDTH)).astype(BF16)
    a = nqk[:, NA_WIDTH:]
    nk_ref[0] = (a * _seg_rms_scale(a, NA_DIM) * gain(G_NK, NA_WIDTH)).astype(BF16)
    _store_chunks(nvt_ref, _nt(wnvt_ref[...], h).astype(BF16))

    dd = proj(C_DIFF, C_DIFF + 2 * DIFF_WIDTH)
    d_scale = DIFF_QK ** -0.5 * LOG2E
    reps = DIFF_WIDTH // LANES
    cdw, sdw = jnp.concatenate([cd] * reps, axis=1), jnp.concatenate([sd] * reps, axis=1)
    for (o_ref, base, grow, grot, sc) in ((dq_ref, 0, G_DQ, G_DQR, d_scale), (dk_ref, DIFF_WIDTH, G_DK, G_DKR, 1.0)):
        a = dd[:, base:base + DIFF_WIDTH]
        b = jnp.concatenate([_rotate_half(a[:, i * LANES:(i + 1) * LANES], DIFF_QK) for i in range(reps)], axis=1)
        r = _seg_rms_scale(a, DIFF_QK)
        o_ref[0] = ((r * sc) * (a * (gain(grow, DIFF_WIDTH) * cdw) + b * (gain(grot, DIFF_WIDTH) * sdw))).astype(BF16)
    _store_chunks(dvt_ref, _nt(wdvt_ref[...], h).astype(BF16))

    g = proj(C_GATE, N_EXT)
    gt_ref[0] = (g / (1.0 + jnp.exp(-g))).astype(BF16)


def _store_chunks(ref, val):
    tk = ref.shape[3]
    for c in range(ref.shape[1]):
        ref[0, c] = val[:, c * tk:(c + 1) * tk]


def _inproj(x, mod_b, gv, w_ext, wuq, wuk, wuvt, wnvt, wdvt, tabs, tm, tk):
    B, S, _ = x.shape
    nt = S // tm
    cm, sm, cd, sd = tabs
    tok = lambda w: pl.BlockSpec((1, tm, w), lambda b, i: (b, i, 0))
    def tr(w, c):
        if c <= tm:
            return pl.BlockSpec((1, tm // c, w, c), lambda b, i: (b, i, 0, 0))
        per = c // tm
        return pl.BlockSpec((1, 1, w, tm), lambda b, i: (b, i // per, 0, i % per))
    tab = pl.BlockSpec((tm, LANES), lambda b, i: (i, 0))
    tok_out = lambda w: jax.ShapeDtypeStruct((B, S, w), BF16)
    tr_out = lambda w, c: jax.ShapeDtypeStruct((B, S // c, w, c), BF16)
    tna = NA_RB * GRID_W
    weights = (gv, w_ext, wuq, wuk, wuvt, wnvt, wdvt)
    return pl.pallas_call(
        _inproj_body,
        out_shape=[tok_out(MLA_HEADS * LANES), tok_out(MLA_HEADS * LANES), tr_out(MLA_WIDTH, tk),
                   tok_out(NA_WIDTH), tok_out(NA_WIDTH), tr_out(NA_WIDTH, tna),
                   tok_out(DIFF_WIDTH), tok_out(DIFF_WIDTH), tr_out(DIFF_WIDTH, tk), tok_out(D_MODEL)],
        grid=(B, nt),
        in_specs=[tok(D_MODEL), pl.BlockSpec((1, 3, D_MODEL), lambda b, i: (b, 0, 0)),
                  *[_resident(a) for a in weights], tab, tab, tab, tab],
        out_specs=[tok(MLA_HEADS * LANES), tok(MLA_HEADS * LANES), tr(MLA_WIDTH, tk),
                   tok(NA_WIDTH), tok(NA_WIDTH), tr(NA_WIDTH, tna),
                   tok(DIFF_WIDTH), tok(DIFF_WIDTH), tr(DIFF_WIDTH, tk), tok(D_MODEL)],
        compiler_params=pltpu.CompilerParams(
            dimension_semantics=("parallel", "parallel"), vmem_limit_bytes=VMEM_LIMIT),
        name="inproj",
    )(x, mod_b, *weights, cm, sm, cd, sd)


def _flash_chains(k_ref, vt_ref, qs, kblk, vhead, nk, m_scr, acc_scr):
    tk = vt_ref.shape[3]
    n = len(qs)

    def operands(j):
        off = j * tk if isinstance(j, int) else pl.multiple_of(j * tk, tk)
        k = k_ref[0, pl.ds(off, tk), :]
        vt = vt_ref[0, j]
        ones = jnp.ones((BF16_SUBLANES, tk), BF16)
        vts = {h: jnp.concatenate([vt[h * HEAD_V:(h + 1) * HEAD_V], ones], axis=0) for h in set(vhead)}
        return k, vts

    def scores(k, g):
        return _nt(k[:, kblk[g] * LANES:(kblk[g] + 1) * LANES], qs[g])

    def issue_ahead(k):
        pending = [scores(k, g) for g in range(min(QK_AHEAD, n))]
        for g in range(n):
            s = pending.pop(0)
            if g + QK_AHEAD < n:
                pending.append(scores(k, g + QK_AHEAD))
            yield g, s

    def exact_tile(j):
        k, vts = operands(j)
        for g, s in issue_ahead(k):
            m = m_scr[g]
            mn = jnp.maximum(m, jnp.max(s, axis=0, keepdims=True))
            p = jnp.exp2(s - mn).astype(BF16)
            acc_scr[g] = acc_scr[g] * jnp.exp2(m - mn) + _mm(vts[vhead[g]], p)
            m_scr[g] = mn

    def lazy_tiles(j0, count, first, worst):
        state = [(m_scr[g], acc_scr[g]) for g in range(n)]
        tiles = [operands(j0 + t) for t in range(count)]
        units = [(t, g) for t in range(count) for g in range(n)]
        pending = [scores(tiles[t][0], g) for t, g in units[:QK_AHEAD]]
        for i, (t, g) in enumerate(units):
            s = pending.pop(0)
            if i + QK_AHEAD < len(units):
                tn, gn = units[i + QK_AHEAD]
                pending.append(scores(tiles[tn][0], gn))
            m, acc = state[g]
            top = jnp.max(s, axis=0, keepdims=True)
            p = jnp.exp2(s - m).astype(BF16)
            mn = jnp.maximum(m, top)
            state[g] = (mn, (acc + _mm(tiles[t][1][vhead[g]], p)) * jnp.exp2(m - mn))
            off = jnp.abs(top - m) if first and t == 0 else top - m
            worst = off if worst is None else jnp.maximum(worst, off)
        for g, (m, acc) in enumerate(state):
            m_scr[g] = m
            acc_scr[g] = acc
        return worst

    def reset(m0):
        m_scr[...] = jnp.full(m_scr.shape, m0, F32)
        acc_scr[...] = jnp.zeros(acc_scr.shape, F32)

    unroll = LAZY_UNROLL if nk % LAZY_UNROLL == 0 else 1
    reset(0.0)
    worst = lazy_tiles(0, unroll, True, None)
    worst = lax.fori_loop(1, nk // unroll, lambda i, w: lazy_tiles(i * unroll, unroll, False, w), worst)

    @pl.when(jnp.max(worst) > LAZY_MAX_RISE)
    def _():
        reset(NEG)

        def step(j, c):
            exact_tile(j)
            return c

        lax.fori_loop(0, nk, step, 0)

    return [acc_scr[g] for g in range(n)]


def _normalised(acc):
    return acc[:HEAD_V] / acc[HEAD_V:HEAD_V + 1]


def _mla_body(q_ref, k_ref, vt_ref, o_ref, m_scr, acc_scr, *, nk):
    heads = list(range(MLA_HEADS))
    qs = [q_ref[0, :, h * LANES:(h + 1) * LANES] for h in heads]
    accs = _flash_chains(k_ref, vt_ref, qs, heads, heads, nk, m_scr, acc_scr)
    o_ref[0] = jnp.concatenate([_normalised(a) for a in accs], axis=0).T.astype(o_ref.dtype)


def _dense_attn_call(body, name, q, k, vt, extra, out_width, tq, chains):
    B, S, wq = q.shape
    nk, wv, tk = vt.shape[1:]
    return pl.pallas_call(
        body,
        out_shape=jax.ShapeDtypeStruct((B, S, out_width), BF16),
        grid=(B, S // tq),
        in_specs=[pl.BlockSpec((1, tq, wq), lambda b, i: (b, i, 0)),
                  pl.BlockSpec((1, S, wq), lambda b, i: (b, 0, 0)),
                  pl.BlockSpec((1, nk, wv, tk), lambda b, i: (b, 0, 0, 0)),
                  *[pl.BlockSpec(a.shape, lambda b, i: (0, 0)) for a in extra]],
        out_specs=pl.BlockSpec((1, tq, out_width), lambda b, i: (b, i, 0)),
        scratch_shapes=[pltpu.VMEM((chains, 1, tq), F32),
                        pltpu.VMEM((chains, HEAD_V + BF16_SUBLANES, tq), F32)],
        compiler_params=pltpu.CompilerParams(
            dimension_semantics=("parallel", "arbitrary"), vmem_limit_bytes=VMEM_LIMIT),
        name=name,
    )(q, k, vt, *extra)


def _mla_attn(q, k, vt, tq):
    return _dense_attn_call(functools.partial(_mla_body, nk=vt.shape[1]), "mla_attn", q, k, vt, (), MLA_WIDTH, tq,
                            MLA_HEADS)


def _diff_body(q_ref, k_ref, vt_ref, lam_ref, sg_ref, o_ref, m_scr, acc_scr, *, nk, lam_init):
    lane = lax.broadcasted_iota(jnp.int32, (1, LANES), 1)
    per_blk = LANES // DIFF_QK
    chains = list(range(2 * DIFF_HEADS))
    qs = []
    for g in chains:
        qb = q_ref[0, :, (g // per_blk) * LANES:(g // per_blk + 1) * LANES]
        lo = (g % per_blk) * DIFF_QK
        qs.append(jnp.where((lane >= lo) & (lane < lo + DIFF_QK), qb, jnp.zeros_like(qb)))
    accs = _flash_chains(k_ref, vt_ref, qs, [g // per_blk for g in chains], [g // 2 for g in chains], nk,
                         m_scr, acc_scr)

    lv = lam_ref[...]
    lam = (jnp.exp(jnp.sum(lv[0:1] * lv[1:2], axis=-1, keepdims=True))
           - jnp.exp(jnp.sum(lv[2:3] * lv[3:4], axis=-1, keepdims=True)) + lam_init)
    outs = []
    for h in range(DIFF_HEADS):
        o = _normalised(accs[2 * h]) - lam * _normalised(accs[2 * h + 1])
        r = lax.rsqrt(jnp.sum(o * o, axis=0, keepdims=True) * (1.0 / DIFF_V) + EPS)
        outs.append(o * r * (sg_ref[...] * (1.0 - lam_init)))
    o_ref[0] = jnp.concatenate(outs, axis=0).T.astype(o_ref.dtype)


def _diff_attn(q, k, vt, lamv, sgt, lam_init, tq):
    body = functools.partial(_diff_body, nk=vt.shape[1], lam_init=lam_init)
    return _dense_attn_call(body, "diff_attn", q, k, vt, (lamv, sgt), DIFF_WIDTH, tq, 2 * DIFF_HEADS)


def _na_body(q_ref, k_ref, vt_ref, bm_ref, o_ref, *, nrb):
    tq = NA_RB * GRID_W
    nkeys = NA_KROWS * GRID_W
    ones = jnp.ones((BF16_SUBLANES, nkeys), BF16)
    lane = lax.broadcasted_iota(jnp.int32, (1, LANES), 1)
    per_blk = LANES // NA_DIM

    def attend(b, shifted):
        rb = pl.program_id(1) * NA_STEP_BLOCKS + b
        variant = jnp.where(rb == 0, 0, jnp.where(rb == nrb - 1, 2, 1))
        chunk0 = jnp.clip(rb - 1, 0, nrb - 3)
        k = k_ref[0, pl.ds(pl.multiple_of(chunk0 * tq, tq), nkeys), :]
        vt = jnp.concatenate([vt_ref[0, chunk0 + c] for c in range(NA_KROWS // NA_RB)], axis=1)

        def scores(h):
            blk = slice((h // per_blk) * LANES, (h // per_blk + 1) * LANES)
            qb = q_ref[0, b * tq:(b + 1) * tq, blk]
            lo = (h % per_blk) * NA_DIM
            qh = jnp.where((lane >= lo) & (lane < lo + NA_DIM), qb, jnp.zeros_like(qb))
            return _nt(k[:, blk], qh) + bm_ref[h, variant]

        pending = [scores(h) for h in range(min(QK_AHEAD, NA_HEADS))]
        outs, worst = [], None
        for h in range(NA_HEADS):
            s = pending.pop(0)
            if h + QK_AHEAD < NA_HEADS:
                pending.append(scores(h + QK_AHEAD))
            top = jnp.max(s, axis=0, keepdims=True)
            p = jnp.exp2(s - top if shifted else s).astype(BF16)
            vt_ext = jnp.concatenate([vt[h * NA_DIM:(h + 1) * NA_DIM], ones], axis=0)
            acc = _mm(vt_ext, p)
            outs.append(acc[:NA_DIM] / acc[NA_DIM:NA_DIM + 1])
            worst = jnp.abs(top) if worst is None else jnp.maximum(worst, jnp.abs(top))
        o_ref[0, b * tq:(b + 1) * tq] = jnp.concatenate(outs, axis=0).T.astype(o_ref.dtype)
        return worst

    def all_blocks(shifted):
        return functools.reduce(jnp.maximum, [attend(b, shifted) for b in range(NA_STEP_BLOCKS)])

    worst = all_blocks(shifted=False)

    @pl.when(jnp.max(worst) > LAZY_MAX_RISE)
    def _():
        all_blocks(shifted=True)


def _na_attn(q, k, vt, bm):
    B, S, _ = q.shape
    tq = NA_RB * GRID_W
    nrb = S // tq
    step = NA_STEP_BLOCKS * tq
    return pl.pallas_call(
        functools.partial(_na_body, nrb=nrb),
        out_shape=jax.ShapeDtypeStruct((B, S, NA_WIDTH), BF16),
        grid=(B, S // step),
        in_specs=[pl.BlockSpec((1, step, NA_WIDTH), lambda b, i: (b, i, 0)),
                  pl.BlockSpec((1, S, NA_WIDTH), lambda b, i: (b, 0, 0)),
                  pl.BlockSpec((1, nrb, NA_WIDTH, tq), lambda b, i: (b, 0, 0, 0)),
                  _resident(bm)],
        out_specs=pl.BlockSpec((1, step, NA_WIDTH), lambda b, i: (b, i, 0)),
        compiler_params=pltpu.CompilerParams(
            dimension_semantics=("parallel", "arbitrary"), vmem_limit_bytes=VMEM_LIMIT),
        name="na_attn",
    )(q, k, vt, bm)


def _na_bias_table(rpb, rows):
    nrb = rows // NA_RB
    W = GRID_W
    H = rpb.shape[0]
    pad_c = W - NA_KC
    rp = jnp.pad(jnp.flip(rpb.astype(F32), axis=-1), ((0, 0), (0, 0), (pad_c, pad_c + 1)))
    skew = jnp.tile(rp, (1, 1, W))[:, :, :W * (2 * W - 1)].reshape(H, -1, W, 2 * W - 1)
    pad_lo, pad_hi = NA_KR // 2, NA_KROWS - NA_KR
    t = jnp.pad(skew[:, :, :, W - 1:] * LOG2E, ((0, 0), (pad_lo, pad_hi), (0, 0), (0, 0)))
    kri = np.arange(NA_KROWS)[:, None, None]
    kc = np.arange(W)[None, :, None]
    c = np.arange(W)[None, None, :]
    c0 = np.clip(c - NA_KC // 2, 0, W - NA_KC)
    col_ok = (kc >= c0) & (kc < c0 + NA_KC)
    variants = []
    for rb in (0, 1, nrb - 1):
        start = NA_RB * min(max(rb - 1, 0), nrb - 3)
        per_row = []
        for ri in range(NA_RB):
            r = NA_RB * rb + ri
            r0 = min(max(r - NA_KR // 2, 0), rows - NA_KR)
            d0 = start - r + (NA_KR - 1) + pad_lo
            assert 0 <= d0 and d0 + NA_KROWS <= t.shape[1]
            ok = ((start + kri >= r0) & (start + kri < r0 + NA_KR)) & col_ok
            per_row.append(jnp.where(ok[None], t[:, d0:d0 + NA_KROWS], NEG))
        variants.append(jnp.concatenate(per_row, axis=-1))
    return jnp.stack(variants, axis=1).reshape(H, 3, NA_KROWS * W, NA_RB * W)


def _outproj_body(x_ref, om_ref, on_ref, od_ref, gt_ref, gate_ref, w_ref, o_ref):
    def branch(o_ref_, a, b):
        mix = (o_ref_[0].astype(F32) * gt_ref[0, :, a:b].astype(F32)).astype(BF16)
        return _mm(mix, w_ref[a:b, :])

    y = (branch(om_ref, 0, MLA_WIDTH) + branch(on_ref, MLA_WIDTH, MLA_WIDTH + NA_WIDTH)
         + branch(od_ref, MLA_WIDTH + NA_WIDTH, D_MODEL))
    o_ref[0] = x_ref[0] + gate_ref[0, 2:3, :] * y


def _outproj(x, om, on, od, gt, mod_b, w_out, tm):
    B, S, _ = x.shape
    tok = lambda w: pl.BlockSpec((1, tm, w), lambda b, i: (b, i, 0))
    return pl.pallas_call(
        _outproj_body,
        out_shape=jax.ShapeDtypeStruct((B, S, D_MODEL), F32),
        grid=(B, S // tm),
        in_specs=[tok(D_MODEL), tok(MLA_WIDTH), tok(NA_WIDTH), tok(DIFF_WIDTH), tok(D_MODEL),
                  pl.BlockSpec((1, 3, D_MODEL), lambda b, i: (b, 0, 0)),
                  _resident(w_out)],
        out_specs=tok(D_MODEL),
        compiler_params=pltpu.CompilerParams(
            dimension_semantics=("parallel", "parallel"), vmem_limit_bytes=VMEM_LIMIT),
        name="outproj",
    )(x, om, on, od, gt, mod_b, w_out)


def _swap_halves(g, half):
    return jnp.concatenate([g[..., half:], g[..., :half]], axis=-1)


def _pad_cols(w, left, total):
    return jnp.pad(w, ((0, 0), (left, total - left - w.shape[-1])))


def _prep_layer(w_in, w_uq, w_ukv, norm_g, q_lat_g, kv_lat_g, mla_q_g, mla_k_g,
                na_q_g, na_k_g, diff_q_g, diff_k_g):
    sizes = (Q_LORA, KV_LORA, MLA_ROPE, 3 * NA_WIDTH, 3 * DIFF_WIDTH, D_MODEL)
    splits = np.cumsum(sizes)[:-1].tolist()
    w_cq, w_ckv, w_kpe, w_na, w_diff, w_gate = jnp.split(w_in, splits, axis=-1)
    hr = MLA_ROPE // 2
    hd = DIFF_QK // 2
    w_qd, w_kd, w_vd = jnp.split(w_diff, 3, axis=-1)
    w_nq, w_nk, w_nv = jnp.split(w_na, 3, axis=-1)
    w_ext = jnp.concatenate([
        w_cq, w_ckv,
        _pad_cols(w_kpe, MLA_NOPE, LANES), w_nq, w_nk, w_qd, w_kd, w_gate], axis=-1).astype(BF16)
    wnvt = w_nv.T.astype(BF16)
    wdvt = w_vd.T.astype(BF16)

    uq = w_uq.reshape(Q_LORA, MLA_HEADS, MLA_QK)
    wuq = jnp.pad(uq, ((0, 0), (0, 0), (0, LANES - MLA_QK))).reshape(Q_LORA, -1).astype(BF16)

    ukv = w_ukv.reshape(KV_LORA, MLA_HEADS, MLA_NOPE + MLA_V)
    wuk = jnp.pad(ukv[..., :MLA_NOPE], ((0, 0), (0, 0), (0, LANES - MLA_NOPE))).reshape(KV_LORA, -1).astype(BF16)
    wuvt = ukv[..., MLA_NOPE:].reshape(KV_LORA, -1).T.astype(BF16)

    def row(v):
        return jnp.pad(v.astype(F32), (0, D_MODEL - v.shape[0]))

    def mla_rows(g):
        rot = jnp.pad(_swap_halves(g[MLA_NOPE:], hr), (MLA_NOPE, 0))
        return row(g), row(rot)

    def diff_rows(g):
        n = DIFF_WIDTH // DIFF_QK
        return row(jnp.tile(g, n)), row(jnp.tile(_swap_halves(g, hd), n))

    rows = [row(norm_g), row(q_lat_g), row(kv_lat_g), *mla_rows(mla_q_g), *mla_rows(mla_k_g),
            row(jnp.tile(na_q_g, NA_HEADS)), row(jnp.tile(na_k_g, NA_HEADS)),
            *diff_rows(diff_q_g), *diff_rows(diff_k_g)]
    rows += [jnp.zeros((D_MODEL,), F32)] * (G_ROWS - len(rows))
    return w_ext, wuq, wuk, wuvt, wnvt, wdvt, jnp.stack(rows)


def _rope_tables(S):
    def cs(dim):
        inv = ROPE_THETA ** (-jnp.arange(0, dim, 2, dtype=F32) / dim)
        ang = jnp.arange(S, dtype=F32)[:, None] * inv[None, :]
        return jnp.cos(ang), jnp.sin(ang)

    c, s = cs(MLA_ROPE)
    pad = LANES - MLA_QK
    cm = jnp.concatenate([jnp.ones((S, MLA_NOPE), F32), c, c, jnp.zeros((S, pad), F32)], axis=-1)
    sm = jnp.concatenate([jnp.zeros((S, MLA_NOPE), F32), s, s, jnp.zeros((S, pad), F32)], axis=-1)
    c, s = cs(DIFF_QK)
    n = LANES // DIFF_QK
    cd = jnp.tile(jnp.concatenate([c, c], axis=-1), (1, n))
    sd = jnp.tile(jnp.concatenate([s, s], axis=-1), (1, n))
    return cm, sm, cd, sd


def _tile(S, pref):
    t = pref
    while S % t:
        t //= 2
    return t


def kernel(x, c, ada_w, ada_b, norm_g, w_in, q_lat_g, w_uq, kv_lat_g, w_ukv, mla_q_g, mla_k_g,
           na_q_g, na_k_g, na_rpb, diff_q_g, diff_k_g, lam_q1, lam_k1, lam_q2, lam_k2, subln_g, w_out):
    B, S, D = x.shape
    L = ada_w.shape[0]
    rows = S // GRID_W
    assert D == D_MODEL and S % (NA_STEP_BLOCKS * NA_RB * GRID_W) == 0 and rows >= NA_KROWS

    tabs = _rope_tables(S)
    mod = _adaln(c, ada_w, ada_b)
    tm = _tile(S, TILE_M)
    tk = _tile(S, TILE_K)
    tqd = _tile(S, TILE_Q_DIFF)

    weights = jax.vmap(_prep_layer)(w_in, w_uq, w_ukv, norm_g, q_lat_g, kv_lat_g, mla_q_g, mla_k_g,
                                    na_q_g, na_k_g, diff_q_g, diff_k_g)
    bias = jax.vmap(functools.partial(_na_bias_table, rows=rows))(na_rpb)
    mod_b = mod.transpose(0, 2, 1, 3)
    lamv = jnp.stack([lam_q1, lam_k1, lam_q2, lam_k2], axis=1).astype(F32)
    sgt = jnp.broadcast_to(subln_g.astype(F32)[:, :, None], (L, DIFF_V, tqd))
    w_out_b = w_out.astype(BF16)

    h = x
    for i in range(L):
        lam_init = 0.8 - 0.6 * math.exp(-0.3 * i)
        w_ext, wuq, wuk, wuvt, wnvt, wdvt, gv = (w[i] for w in weights)
        qm, km, vmt, nq, nk, nvt, dq, dk, dvt, gt = _inproj(h, mod_b[i], gv, w_ext, wuq, wuk, wuvt, wnvt, wdvt,
                                                            tabs, tm, tk)
        om = _mla_attn(qm, km, vmt, _tile(S, TILE_Q_MLA))
        on = _na_attn(nq, nk, nvt, bias[i])
        od = _diff_attn(dq, dk, dvt, lamv[i], sgt[i], lam_init, tqd)
        h = _outproj(h, om, on, od, gt, mod_b[i], w_out_b[i], tm)
    return h
```

```python
import functools
import math

import numpy as np
import jax
import jax.numpy as jnp
from jax import lax
from jax.experimental import pallas as pl
from jax.experimental.pallas import tpu as pltpu

F32 = jnp.float32
BF16 = jnp.bfloat16

D_MODEL = 1024
GRID_W = 64
EPS = 1e-6
ROPE_THETA = 10000.0
MLA_HEADS, MLA_NOPE, MLA_ROPE, MLA_V = 6, 64, 32, 64
MLA_QK = MLA_NOPE + MLA_ROPE
Q_LORA, KV_LORA = 256, 128
NA_HEADS, NA_DIM, NA_KR, NA_KC = 6, 64, 8, 16
DIFF_HEADS, DIFF_QK = 4, 32
DIFF_V = 2 * DIFF_QK
MLA_WIDTH, NA_WIDTH, DIFF_WIDTH = MLA_HEADS * MLA_V, NA_HEADS * NA_DIM, DIFF_HEADS * DIFF_V
HEAD_V = 64

LANES = 128
BF16_SUBLANES = 16
MXU_N = 256
VMEM_LIMIT = 52 * 1024 * 1024

C_CQ, C_CKV, C_KPE = 0, 256, 384
C_NA = 512
C_DIFF = 1280
C_GATE = 1792
N_EXT = 2816

G_NORM, G_QLAT, G_KVLAT, G_MQ, G_MQR, G_MK, G_MKR, G_NQ, G_NK, G_DQ, G_DQR, G_DK, G_DKR = range(13)
G_ROWS = 16

TILE_M = 1024
TILE_Q_MLA = 1024
TILE_Q_DIFF = 1024
TILE_K = 512
LAZY_UNROLL = 1
QK_AHEAD = 3

NA_RB = 4
NA_STEP_BLOCKS = 4
NA_KROWS = 12
NEG = -1e30
LOG2E = math.log2(math.e)
LAZY_MAX_RISE = 64.0


def _nt(a, b):
    return lax.dot_general(a, b, (((1,), (1,)), ((), ())), preferred_element_type=F32)


def _mm(a, b):
    return jnp.dot(a, b, preferred_element_type=F32)


def _seg_ones(seg, width):
    sh = int(math.log2(seg))
    r = lax.broadcasted_iota(jnp.int32, (width, width), 0) >> sh
    c = lax.broadcasted_iota(jnp.int32, (width, width), 1) >> sh
    return jnp.where(r == c, 1.0, 0.0).astype(BF16)


def _seg_rms_scale(x, seg):
    x2 = (x * x).astype(BF16)
    width = x.shape[1]
    parts = []
    for a in range(0, width, MXU_N):
        w = min(MXU_N, width - a)
        parts.append(_mm(x2[:, a:a + w], _seg_ones(seg, w)))
    ss = parts[0] if len(parts) == 1 else jnp.concatenate(parts, axis=1)
    return lax.rsqrt(ss * (1.0 / seg) + EPS)


def _rotate_half(x, group):
    half = group // 2
    lane = lax.broadcasted_iota(jnp.int32, (1, LANES), 1)
    first = (lane & (group - 1)) < half
    from_above = pltpu.roll(x, LANES - half, 1)
    from_below = pltpu.roll(x, half, 1)
    return jnp.where(first, -from_above, from_below)


def _resident(a):
    zeros = (0,) * a.ndim
    return pl.BlockSpec(a.shape, lambda *_: zeros, pipeline_mode=pl.Buffered(1))


def _adaln_body(c_ref, w_ref, b_ref, o_ref):
    c = c_ref[...]
    a = c / (1.0 + jnp.exp(-c))
    o_ref[0, 0] = jnp.dot(a, w_ref[0], preferred_element_type=F32,
                          precision=lax.Precision.HIGHEST) + b_ref[0, 0]


def _adaln(c, ada_w, ada_b):
    L = ada_w.shape[0]
    B = c.shape[0]
    b4 = ada_b.reshape(L, 3, 1, D_MODEL)
    return pl.pallas_call(
        _adaln_body,
        out_shape=jax.ShapeDtypeStruct((L, 3, B, D_MODEL), F32),
        grid=(L, 3),
        in_specs=[
            pl.BlockSpec((B, D_MODEL), lambda l, j: (0, 0)),
            pl.BlockSpec((1, D_MODEL, D_MODEL), lambda l, j: (l, 0, j)),
            pl.BlockSpec((1, 1, 1, D_MODEL), lambda l, j: (l, j, 0, 0)),
        ],
        out_specs=pl.BlockSpec((1, 1, B, D_MODEL), lambda l, j: (l, j, 0, 0)),
        compiler_params=pltpu.CompilerParams(
            dimension_semantics=("arbitrary", "arbitrary"), vmem_limit_bytes=VMEM_LIMIT),
        name="adaln",
    )(c, ada_w, b4)


def _inproj_body(x_ref, mod_ref, gv_ref, w_ref, wuq_ref, wuk_ref, wuvt_ref, wnvt_ref, wdvt_ref,
                 cm_ref, sm_ref, cd_ref, sd_ref,
                 qm_ref, km_ref, vmt_ref, nq_ref, nk_ref, nvt_ref, dq_ref, dk_ref, dvt_ref, gt_ref):
    def gain(row, width, off=0):
        return gv_ref[row:row + 1, off:off + width]

    x = x_ref[0]
    ms = jnp.mean(x * x, axis=-1, keepdims=True)
    xn = x * lax.rsqrt(ms + EPS) * gain(G_NORM, D_MODEL)
    h = (xn * (1.0 + mod_ref[0, 1:2, :]) + mod_ref[0, 0:1, :]).astype(BF16)

    def proj(a, b):
        return _mm(h, w_ref[:, a:b])

    def lat_norm(v, row, width):
        r = lax.rsqrt(jnp.mean(v * v, axis=-1, keepdims=True) + EPS)
        return (v * r * gain(row, width)).astype(BF16)

    cm, sm = cm_ref[...], sm_ref[...]
    cd, sd = cd_ref[...], sd_ref[...]

    lat = proj(C_CQ, C_NA)
    cqn = lat_norm(lat[:, C_CQ:C_CQ + Q_LORA], G_QLAT, Q_LORA)
    qq = _mm(cqn, wuq_ref[...])
    wq_c = gain(G_MQ, LANES) * cm
    wq_s = gain(G_MQR, LANES) * sm
    q_scale = MLA_QK ** -0.5 * LOG2E
    for hd in range(MLA_HEADS):
        a = qq[:, hd * LANES:(hd + 1) * LANES]
        r = lax.rsqrt(jnp.sum(a * a, axis=-1, keepdims=True) * (1.0 / MLA_QK) + EPS)
        rotated = _rotate_half(a, MLA_ROPE)
        qm_ref[0, :, hd * LANES:(hd + 1) * LANES] = ((r * q_scale) * (a * wq_c + rotated * wq_s)).astype(BF16)

    ckvn = lat_norm(lat[:, C_CKV:C_CKV + KV_LORA], G_KVLAT, KV_LORA)
    _store_chunks(vmt_ref, _nt(wuvt_ref[...], ckvn).astype(BF16))
    kn = _mm(ckvn, wuk_ref[...])
    kpe = lat[:, C_KPE:C_KPE + LANES]
    wk_c = gain(G_MK, LANES) * cm
    k_rot = _rotate_half(kpe, MLA_ROPE) * (gain(G_MKR, LANES) * sm)
    for hd in range(MLA_HEADS):
        a = kn[:, hd * LANES:(hd + 1) * LANES] + kpe
        r = lax.rsqrt(jnp.sum(a * a, axis=-1, keepdims=True) * (1.0 / MLA_QK) + EPS)
        km_ref[0, :, hd * LANES:(hd + 1) * LANES] = (r * (a * wk_c + k_rot)).astype(BF16)

    nqk = proj(C_NA, C_NA + 2 * NA_WIDTH)
    na_scale = NA_DIM ** -0.5 * LOG2E
    a = nqk[:, :NA_WIDTH]
    nq_ref[0] = (a * (_seg_rms_scale(a, NA_DIM) * na_scale) * gain(G_NQ, NA_WIDTH)).astype(BF16)
    a = nqk[:, NA_WIDTH:]
    nk_ref[0] = (a * _seg_rms_scale(a, NA_DIM) * gain(G_NK, NA_WIDTH)).astype(BF16)
    _store_chunks(nvt_ref, _nt(wnvt_ref[...], h).astype(BF16))

    dd = proj(C_DIFF, C_DIFF + 2 * DIFF_WIDTH)
    d_scale = DIFF_QK ** -0.5 * LOG2E
    reps = DIFF_WIDTH // LANES
    cdw, sdw = jnp.concatenate([cd] * reps, axis=1), jnp.concatenate([sd] * reps, axis=1)
    for (o_ref, base, grow, grot, sc) in ((dq_ref, 0, G_DQ, G_DQR, d_scale), (dk_ref, DIFF_WIDTH, G_DK, G_DKR, 1.0)):
        a = dd[:, base:base + DIFF_WIDTH]
        b = jnp.concatenate([_rotate_half(a[:, i * LANES:(i + 1) * LANES], DIFF_QK) for i in range(reps)], axis=1)
        r = _seg_rms_scale(a, DIFF_QK)
        o_ref[0] = ((r * sc) * (a * (gain(grow, DIFF_WIDTH) * cdw) + b * (gain(grot, DIFF_WIDTH) * sdw))).astype(BF16)
    _store_chunks(dvt_ref, _nt(wdvt_ref[...], h).astype(BF16))

    g = proj(C_GATE, N_EXT)
    gt_ref[0] = (g / (1.0 + jnp.exp(-g))).astype(BF16)


def _store_chunks(ref, val):
    tk = ref.shape[3]
    for c in range(ref.shape[1]):
        ref[0, c] = val[:, c * tk:(c + 1) * tk]


def _inproj(x, mod_b, gv, w_ext, wuq, wuk, wuvt, wnvt, wdvt, tabs, tm, tk):
    B, S, _ = x.shape
    nt = S // tm
    cm, sm, cd, sd = tabs
    tok = lambda w: pl.BlockSpec((1, tm, w), lambda b, i: (b, i, 0))
    def tr(w, c):
        if c <= tm:
            return pl.BlockSpec((1, tm // c, w, c), lambda b, i: (b, i, 0, 0))
        per = c // tm
        return pl.BlockSpec((1, 1, w, tm), lambda b, i: (b, i // per, 0, i % per))
    tab = pl.BlockSpec((tm, LANES), lambda b, i: (i, 0))
    tok_out = lambda w: jax.ShapeDtypeStruct((B, S, w), BF16)
    tr_out = lambda w, c: jax.ShapeDtypeStruct((B, S // c, w, c), BF16)
    tna = NA_RB * GRID_W
    weights = (gv, w_ext, wuq, wuk, wuvt, wnvt, wdvt)
    return pl.pallas_call(
        _inproj_body,
        out_shape=[tok_out(MLA_HEADS * LANES), tok_out(MLA_HEADS * LANES), tr_out(MLA_WIDTH, tk),
                   tok_out(NA_WIDTH), tok_out(NA_WIDTH), tr_out(NA_WIDTH, tna),
                   tok_out(DIFF_WIDTH), tok_out(DIFF_WIDTH), tr_out(DIFF_WIDTH, tk), tok_out(D_MODEL)],
        grid=(B, nt),
        in_specs=[tok(D_MODEL), pl.BlockSpec((1, 3, D_MODEL), lambda b, i: (b, 0, 0)),
                  *[_resident(a) for a in weights], tab, tab, tab, tab],
        out_specs=[tok(MLA_HEADS * LANES), tok(MLA_HEADS * LANES), tr(MLA_WIDTH, tk),
                   tok(NA_WIDTH), tok(NA_WIDTH), tr(NA_WIDTH, tna),
                   tok(DIFF_WIDTH), tok(DIFF_WIDTH), tr(DIFF_WIDTH, tk), tok(D_MODEL)],
        compiler_params=pltpu.CompilerParams(
            dimension_semantics=("parallel", "parallel"), vmem_limit_bytes=VMEM_LIMIT),
        name="inproj",
    )(x, mod_b, *weights, cm, sm, cd, sd)


def _flash_chains(k_ref, vt_ref, qs, kblk, vhead, nk, m_scr, acc_scr):
    tk = vt_ref.shape[3]
    n = len(qs)

    def operands(j):
        off = j * tk if isinstance(j, int) else pl.multiple_of(j * tk, tk)
        k = k_ref[0, pl.ds(off, tk), :]
        vt = vt_ref[0, j]
        ones = jnp.ones((BF16_SUBLANES, tk), BF16)
        vts = {h: jnp.concatenate([vt[h * HEAD_V:(h + 1) * HEAD_V], ones], axis=0) for h in set(vhead)}
        return k, vts

    def scores(k, g):
        return _nt(k[:, kblk[g] * LANES:(kblk[g] + 1) * LANES], qs[g])

    def issue_ahead(k):
        pending = [scores(k, g) for g in range(min(QK_AHEAD, n))]
        for g in range(n):
            s = pending.pop(0)
            if g + QK_AHEAD < n:
                pending.append(scores(k, g + QK_AHEAD))
            yield g, s

    def exact_tile(j):
        k, vts = operands(j)
        for g, s in issue_ahead(k):
            m = m_scr[g]
            mn = jnp.maximum(m, jnp.max(s, axis=0, keepdims=True))
            p = jnp.exp2(s - mn).astype(BF16)
            acc_scr[g] = acc_scr[g] * jnp.exp2(m - mn) + _mm(vts[vhead[g]], p)
            m_scr[g] = mn

    def lazy_tiles(j0, count, first, worst):
        state = [(m_scr[g], acc_scr[g]) for g in range(n)]
        tiles = [operands(j0 + t) for t in range(count)]
        units = [(t, g) for t in range(count) for g in range(n)]
        pending = [scores(tiles[t][0], g) for t, g in units[:QK_AHEAD]]
        for i, (t, g) in enumerate(units):
            s = pending.pop(0)
            if i + QK_AHEAD < len(units):
                tn, gn = units[i + QK_AHEAD]
                pending.append(scores(tiles[tn][0], gn))
            m, acc = state[g]
            top = jnp.max(s, axis=0, keepdims=True)
            p = jnp.exp2(s - m).astype(BF16)
            mn = jnp.maximum(m, top)
            state[g] = (mn, (acc + _mm(tiles[t][1][vhead[g]], p)) * jnp.exp2(m - mn))
            off = jnp.abs(top - m) if first and t == 0 else top - m
            worst = off if worst is None else jnp.maximum(worst, off)
        for g, (m, acc) in enumerate(state):
            m_scr[g] = m
            acc_scr[g] = acc
        return worst

    def reset(m0):
        m_scr[...] = jnp.full(m_scr.shape, m0, F32)
        acc_scr[...] = jnp.zeros(acc_scr.shape, F32)

    unroll = LAZY_UNROLL if nk % LAZY_UNROLL == 0 else 1
    reset(0.0)
    worst = lazy_tiles(0, unroll, True, None)
    worst = lax.fori_loop(1, nk // unroll, lambda i, w: lazy_tiles(i * unroll, unroll, False, w), worst)

    @pl.when(jnp.max(worst) > LAZY_MAX_RISE)
    def _():
        reset(NEG)

        def step(j, c):
            exact_tile(j)
            return c

        lax.fori_loop(0, nk, step, 0)

    return [acc_scr[g] for g in range(n)]


def _normalised(acc):
    return acc[:HEAD_V] / acc[HEAD_V:HEAD_V + 1]


def _mla_body(q_ref, k_ref, vt_ref, o_ref, m_scr, acc_scr, *, nk):
    heads = list(range(MLA_HEADS))
    qs = [q_ref[0, :, h * LANES:(h + 1) * LANES] for h in heads]
    accs = _flash_chains(k_ref, vt_ref, qs, heads, heads, nk, m_scr, acc_scr)
    o_ref[0] = jnp.concatenate([_normalised(a) for a in accs], axis=0).T.astype(o_ref.dtype)


def _dense_attn_call(body, name, q, k, vt, extra, out_width, tq, chains):
    B, S, wq = q.shape
    nk, wv, tk = vt.shape[1:]
    return pl.pallas_call(
        body,
        out_shape=jax.ShapeDtypeStruct((B, S, out_width), BF16),
        grid=(B, S // tq),
        in_specs=[pl.BlockSpec((1, tq, wq), lambda b, i: (b, i, 0)),
                  pl.BlockSpec((1, S, wq), lambda b, i: (b, 0, 0)),
                  pl.BlockSpec((1, nk, wv, tk), lambda b, i: (b, 0, 0, 0)),
                  *[pl.BlockSpec(a.shape, lambda b, i: (0, 0)) for a in extra]],
        out_specs=pl.BlockSpec((1, tq, out_width), lambda b, i: (b, i, 0)),
        scratch_shapes=[pltpu.VMEM((chains, 1, tq), F32),
                        pltpu.VMEM((chains, HEAD_V + BF16_SUBLANES, tq), F32)],
        compiler_params=pltpu.CompilerParams(
            dimension_semantics=("parallel", "arbitrary"), vmem_limit_bytes=VMEM_LIMIT),
        name=name,
    )(q, k, vt, *extra)


def _mla_attn(q, k, vt, tq):
    return _dense_attn_call(functools.partial(_mla_body, nk=vt.shape[1]), "mla_attn", q, k, vt, (), MLA_WIDTH, tq,
                            MLA_HEADS)


def _diff_body(q_ref, k_ref, vt_ref, lam_ref, sg_ref, o_ref, m_scr, acc_scr, *, nk, lam_init):
    lane = lax.broadcasted_iota(jnp.int32, (1, LANES), 1)
    per_blk = LANES // DIFF_QK
    chains = list(range(2 * DIFF_HEADS))
    qs = []
    for g in chains:
        qb = q_ref[0, :, (g // per_blk) * LANES:(g // per_blk + 1) * LANES]
        lo = (g % per_blk) * DIFF_QK
        qs.append(jnp.where((lane >= lo) & (lane < lo + DIFF_QK), qb, jnp.zeros_like(qb)))
    accs = _flash_chains(k_ref, vt_ref, qs, [g // per_blk for g in chains], [g // 2 for g in chains], nk,
                         m_scr, acc_scr)

    lv = lam_ref[...]
    lam = (jnp.exp(jnp.sum(lv[0:1] * lv[1:2], axis=-1, keepdims=True))
           - jnp.exp(jnp.sum(lv[2:3] * lv[3:4], axis=-1, keepdims=True)) + lam_init)
    outs = []
    for h in range(DIFF_HEADS):
        o = _normalised(accs[2 * h]) - lam * _normalised(accs[2 * h + 1])
        r = lax.rsqrt(jnp.sum(o * o, axis=0, keepdims=True) * (1.0 / DIFF_V) + EPS)
        outs.append(o * r * (sg_ref[...] * (1.0 - lam_init)))
    o_ref[0] = jnp.concatenate(outs, axis=0).T.astype(o_ref.dtype)


def _diff_attn(q, k, vt, lamv, sgt, lam_init, tq):
    body = functools.partial(_diff_body, nk=vt.shape[1], lam_init=lam_init)
    return _dense_attn_call(body, "diff_attn", q, k, vt, (lamv, sgt), DIFF_WIDTH, tq, 2 * DIFF_HEADS)


def _na_body(q_ref, k_ref, vt_ref, bm_ref, o_ref, *, nrb):
    tq = NA_RB * GRID_W
    nkeys = NA_KROWS * GRID_W
    ones = jnp.ones((BF16_SUBLANES, nkeys), BF16)
    lane = lax.broadcasted_iota(jnp.int32, (1, LANES), 1)
    per_blk = LANES // NA_DIM

    def attend(b, shifted):
        rb = pl.program_id(1) * NA_STEP_BLOCKS + b
        variant = jnp.where(rb == 0, 0, jnp.where(rb == nrb - 1, 2, 1))
        chunk0 = jnp.clip(rb - 1, 0, nrb - 3)
        k = k_ref[0, pl.ds(pl.multiple_of(chunk0 * tq, tq), nkeys), :]
        vt = jnp.concatenate([vt_ref[0, chunk0 + c] for c in range(NA_KROWS // NA_RB)], axis=1)

        def scores(h):
            blk = slice((h // per_blk) * LANES, (h // per_blk + 1) * LANES)
            qb = q_ref[0, b * tq:(b + 1) * tq, blk]
            lo = (h % per_blk) * NA_DIM
            qh = jnp.where((lane >= lo) & (lane < lo + NA_DIM), qb, jnp.zeros_like(qb))
            return _nt(k[:, blk], qh) + bm_ref[h, variant]

        pending = [scores(h) for h in range(min(QK_AHEAD, NA_HEADS))]
        outs, worst = [], None
        for h in range(NA_HEADS):
            s = pending.pop(0)
            if h + QK_AHEAD < NA_HEADS:
                pending.append(scores(h + QK_AHEAD))
            top = jnp.max(s, axis=0, keepdims=True)
            p = jnp.exp2(s - top if shifted else s).astype(BF16)
            vt_ext = jnp.concatenate([vt[h * NA_DIM:(h + 1) * NA_DIM], ones], axis=0)
            acc = _mm(vt_ext, p)
            outs.append(acc[:NA_DIM] / acc[NA_DIM:NA_DIM + 1])
            worst = jnp.abs(top) if worst is None else jnp.maximum(worst, jnp.abs(top))
        o_ref[0, b * tq:(b + 1) * tq] = jnp.concatenate(outs, axis=0).T.astype(o_ref.dtype)
        return worst

    def all_blocks(shifted):
        return functools.reduce(jnp.maximum, [attend(b, shifted) for b in range(NA_STEP_BLOCKS)])

    worst = all_blocks(shifted=False)

    @pl.when(jnp.max(worst) > LAZY_MAX_RISE)
    def _():
        all_blocks(shifted=True)


def _na_attn(q, k, vt, bm):
    B, S, _ = q.shape
    tq = NA_RB * GRID_W
    nrb = S // tq
    step = NA_STEP_BLOCKS * tq
    return pl.pallas_call(
        functools.partial(_na_body, nrb=nrb),
        out_shape=jax.ShapeDtypeStruct((B, S, NA_WIDTH), BF16),
        grid=(B, S // step),
        in_specs=[pl.BlockSpec((1, step, NA_WIDTH), lambda b, i: (b, i, 0)),
                  pl.BlockSpec((1, S, NA_WIDTH), lambda b, i: (b, 0, 0)),
                  pl.BlockSpec((1, nrb, NA_WIDTH, tq), lambda b, i: (b, 0, 0, 0)),
                  _resident(bm)],
        out_specs=pl.BlockSpec((1, step, NA_WIDTH), lambda b, i: (b, i, 0)),
        compiler_params=pltpu.CompilerParams(
            dimension_semantics=("parallel", "arbitrary"), vmem_limit_bytes=VMEM_LIMIT),
        name="na_attn",
    )(q, k, vt, bm)


def _na_bias_table(rpb, rows):
    nrb = rows // NA_RB
    W = GRID_W
    H = rpb.shape[0]
    pad_c = W - NA_KC
    rp = jnp.pad(jnp.flip(rpb.astype(F32), axis=-1), ((0, 0), (0, 0), (pad_c, pad_c + 1)))
    skew = jnp.tile(rp, (1, 1, W))[:, :, :W * (2 * W - 1)].reshape(H, -1, W, 2 * W - 1)
    pad_lo, pad_hi = NA_KR // 2, NA_KROWS - NA_KR
    t = jnp.pad(skew[:, :, :, W - 1:] * LOG2E, ((0, 0), (pad_lo, pad_hi), (0, 0), (0, 0)))
    kri = np.arange(NA_KROWS)[:, None, None]
    kc = np.arange(W)[None, :, None]
    c = np.arange(W)[None, None, :]
    c0 = np.clip(c - NA_KC // 2, 0, W - NA_KC)
    col_ok = (kc >= c0) & (kc < c0 + NA_KC)
    variants = []
    for rb in (0, 1, nrb - 1):
        start = NA_RB * min(max(rb - 1, 0), nrb - 3)
        per_row = []
        for ri in range(NA_RB):
            r = NA_RB * rb + ri
            r0 = min(max(r - NA_KR // 2, 0), rows - NA_KR)
            d0 = start - r + (NA_KR - 1) + pad_lo
            assert 0 <= d0 and d0 + NA_KROWS <= t.shape[1]
            ok = ((start + kri >= r0) & (start + kri < r0 + NA_KR)) & col_ok
            per_row.append(jnp.where(ok[None], t[:, d0:d0 + NA_KROWS], NEG))
        variants.append(jnp.concatenate(per_row, axis=-1))
    return jnp.stack(variants, axis=1).reshape(H, 3, NA_KROWS * W, NA_RB * W)


def _outproj_body(x_ref, om_ref, on_ref, od_ref, gt_ref, gate_ref, w_ref, o_ref):
    def branch(o_ref_, a, b):
        mix = (o_ref_[0].astype(F32) * gt_ref[0, :, a:b].astype(F32)).astype(BF16)
        return _mm(mix, w_ref[a:b, :])

    y = (branch(om_ref, 0, MLA_WIDTH) + branch(on_ref, MLA_WIDTH, MLA_WIDTH + NA_WIDTH)
         + branch(od_ref, MLA_WIDTH + NA_WIDTH, D_MODEL))
    o_ref[0] = x_ref[0] + gate_ref[0, 2:3, :] * y


def _outproj(x, om, on, od, gt, mod_b, w_out, tm):
    B, S, _ = x.shape
    tok = lambda w: pl.BlockSpec((1, tm, w), lambda b, i: (b, i, 0))
    return pl.pallas_call(
        _outproj_body,
        out_shape=jax.ShapeDtypeStruct((B, S, D_MODEL), F32),
        grid=(B, S // tm),
        in_specs=[tok(D_MODEL), tok(MLA_WIDTH), tok(NA_WIDTH), tok(DIFF_WIDTH), tok(D_MODEL),
                  pl.BlockSpec((1, 3, D_MODEL), lambda b, i: (b, 0, 0)),
                  _resident(w_out)],
        out_specs=tok(D_MODEL),
        compiler_params=pltpu.CompilerParams(
            dimension_semantics=("parallel", "parallel"), vmem_limit_bytes=VMEM_LIMIT),
        name="outproj",
    )(x, om, on, od, gt, mod_b, w_out)


def _swap_halves(g, half):
    return jnp.concatenate([g[..., half:], g[..., :half]], axis=-1)


def _pad_cols(w, left, total):
    return jnp.pad(w, ((0, 0), (left, total - left - w.shape[-1])))


def _prep_layer(w_in, w_uq, w_ukv, norm_g, q_lat_g, kv_lat_g, mla_q_g, mla_k_g,
                na_q_g, na_k_g, diff_q_g, diff_k_g):
    sizes = (Q_LORA, KV_LORA, MLA_ROPE, 3 * NA_WIDTH, 3 * DIFF_WIDTH, D_MODEL)
    splits = np.cumsum(sizes)[:-1].tolist()
    w_cq, w_ckv, w_kpe, w_na, w_diff, w_gate = jnp.split(w_in, splits, axis=-1)
    hr = MLA_ROPE // 2
    hd = DIFF_QK // 2
    w_qd, w_kd, w_vd = jnp.split(w_diff, 3, axis=-1)
    w_nq, w_nk, w_nv = jnp.split(w_na, 3, axis=-1)
    w_ext = jnp.concatenate([
        w_cq, w_ckv,
        _pad_cols(w_kpe, MLA_NOPE, LANES), w_nq, w_nk, w_qd, w_kd, w_gate], axis=-1).astype(BF16)
    wnvt = w_nv.T.astype(BF16)
    wdvt = w_vd.T.astype(BF16)

    uq = w_uq.reshape(Q_LORA, MLA_HEADS, MLA_QK)
    wuq = jnp.pad(uq, ((0, 0), (0, 0), (0, LANES - MLA_QK))).reshape(Q_LORA, -1).astype(BF16)

    ukv = w_ukv.reshape(KV_LORA, MLA_HEADS, MLA_NOPE + MLA_V)
    wuk = jnp.pad(ukv[..., :MLA_NOPE], ((0, 0), (0, 0), (0, LANES - MLA_NOPE))).reshape(KV_LORA, -1).astype(BF16)
    wuvt = ukv[..., MLA_NOPE:].reshape(KV_LORA, -1).T.astype(BF16)

    def row(v):
        return jnp.pad(v.astype(F32), (0, D_MODEL - v.shape[0]))

    def mla_rows(g):
        rot = jnp.pad(_swap_halves(g[MLA_NOPE:], hr), (MLA_NOPE, 0))
        return row(g), row(rot)

    def diff_rows(g):
        n = DIFF_WIDTH // DIFF_QK
        return row(jnp.tile(g, n)), row(jnp.tile(_swap_halves(g, hd), n))

    rows = [row(norm_g), row(q_lat_g), row(kv_lat_g), *mla_rows(mla_q_g), *mla_rows(mla_k_g),
            row(jnp.tile(na_q_g, NA_HEADS)), row(jnp.tile(na_k_g, NA_HEADS)),
            *diff_rows(diff_q_g), *diff_rows(diff_k_g)]
    rows += [jnp.zeros((D_MODEL,), F32)] * (G_ROWS - len(rows))
    return w_ext, wuq, wuk, wuvt, wnvt, wdvt, jnp.stack(rows)


def _rope_tables(S):
    def cs(dim):
        inv = ROPE_THETA ** (-jnp.arange(0, dim, 2, dtype=F32) / dim)
        ang = jnp.arange(S, dtype=F32)[:, None] * inv[None, :]
        return jnp.cos(ang), jnp.sin(ang)

    c, s = cs(MLA_ROPE)
    pad = LANES - MLA_QK
    cm = jnp.concatenate([jnp.ones((S, MLA_NOPE), F32), c, c, jnp.zeros((S, pad), F32)], axis=-1)
    sm = jnp.concatenate([jnp.zeros((S, MLA_NOPE), F32), s, s, jnp.zeros((S, pad), F32)], axis=-1)
    c, s = cs(DIFF_QK)
    n = LANES // DIFF_QK
    cd = jnp.tile(jnp.concatenate([c, c], axis=-1), (1, n))
    sd = jnp.tile(jnp.concatenate([s, s], axis=-1), (1, n))
    return cm, sm, cd, sd


def _tile(S, pref):
    t = pref
    while S % t:
        t //= 2
    return t


def kernel(x, c, ada_w, ada_b, norm_g, w_in, q_lat_g, w_uq, kv_lat_g, w_ukv, mla_q_g, mla_k_g,
           na_q_g, na_k_g, na_rpb, diff_q_g, diff_k_g, lam_q1, lam_k1, lam_q2, lam_k2, subln_g, w_out):
    B, S, D = x.shape
    L = ada_w.shape[0]
    rows = S // GRID_W
    assert D == D_MODEL and S % (NA_STEP_BLOCKS * NA_RB * GRID_W) == 0 and rows >= NA_KROWS

    tabs = _rope_tables(S)
    mod = _adaln(c, ada_w, ada_b)
    tm = _tile(S, TILE_M)
    tk = _tile(S, TILE_K)
    tqd = _tile(S, TILE_Q_DIFF)

    weights = jax.vmap(_prep_layer)(w_in, w_uq, w_ukv, norm_g, q_lat_g, kv_lat_g, mla_q_g, mla_k_g,
                                    na_q_g, na_k_g, diff_q_g, diff_k_g)
    bias = jax.vmap(functools.partial(_na_bias_table, rows=rows))(na_rpb)
    mod_b = mod.transpose(0, 2, 1, 3)
    lamv = jnp.stack([lam_q1, lam_k1, lam_q2, lam_k2], axis=1).astype(F32)
    sgt = jnp.broadcast_to(subln_g.astype(F32)[:, :, None], (L, DIFF_V, tqd))
    w_out_b = w_out.astype(BF16)

    h = x
    for i in range(L):
        lam_init = 0.8 - 0.6 * math.exp(-0.3 * i)
        w_ext, wuq, wuk, wuvt, wnvt, wdvt, gv = (w[i] for w in weights)
        qm, km, vmt, nq, nk, nvt, dq, dk, dvt, gt = _inproj(h, mod_b[i], gv, w_ext, wuq, wuk, wuvt, wnvt, wdvt,
                                                            tabs, tm, tk)
        om = _mla_attn(qm, km, vmt, _tile(S, TILE_Q_MLA))
        on = _na_attn(nq, nk, nvt, bias[i])
        od = _diff_attn(dq, dk, dvt, lamv[i], sgt[i], lam_init, tqd)
        h = _outproj(h, om, on, od, gt, mod_b[i], w_out_b[i], tm)
    return h
```

```python
import functools
import math

import numpy as np
import jax
import jax.numpy as jnp
from jax import lax
from jax.experimental import pallas as pl
from jax.experimental.pallas import tpu as pltpu

F32 = jnp.float32
BF16 = jnp.bfloat16

D_MODEL = 1024
GRID_W = 64
EPS = 1e-6
ROPE_THETA = 10000.0
MLA_HEADS, MLA_NOPE, MLA_ROPE, MLA_V = 6, 64, 32, 64
MLA_QK = MLA_NOPE + MLA_ROPE
Q_LORA, KV_LORA = 256, 128
NA_HEADS, NA_DIM, NA_KR, NA_KC = 6, 64, 8, 16
DIFF_HEADS, DIFF_QK = 4, 32
DIFF_V = 2 * DIFF_QK
MLA_WIDTH, NA_WIDTH, DIFF_WIDTH = MLA_HEADS * MLA_V, NA_HEADS * NA_DIM, DIFF_HEADS * DIFF_V
HEAD_V = 64

LANES = 128
BF16_SUBLANES = 16
MXU_N = 256
VMEM_LIMIT = 52 * 1024 * 1024

C_CQ, C_CKV, C_KPE = 0, 256, 384
C_NA = 512
C_DIFF = 1280
C_GATE = 1792
N_EXT = 2816

G_NORM, G_QLAT, G_KVLAT, G_MQ, G_MQR, G_MK, G_MKR, G_NQ, G_NK, G_DQ, G_DQR, G_DK, G_DKR = range(13)
G_ROWS = 16

TILE_M = 1024
TILE_Q_MLA = 1024
TILE_Q_DIFF = 1024
TILE_K = 512
LAZY_UNROLL = 1
QK_AHEAD = 5

NA_RB = 4
NA_STEP_BLOCKS = 4
NA_KROWS = 12
NEG = -1e30
LOG2E = math.log2(math.e)
LAZY_MAX_RISE = 64.0


def _nt(a, b):
    return lax.dot_general(a, b, (((1,), (1,)), ((), ())), preferred_element_type=F32)


def _mm(a, b):
    return jnp.dot(a, b, preferred_element_type=F32)


def _seg_ones(seg, width):
    sh = int(math.log2(seg))
    r = lax.broadcasted_iota(jnp.int32, (width, width), 0) >> sh
    c = lax.broadcasted_iota(jnp.int32, (width, width), 1) >> sh
    return jnp.where(r == c, 1.0, 0.0).astype(BF16)


def _seg_rms_scale(x, seg):
    x2 = (x * x).astype(BF16)
    width = x.shape[1]
    parts = []
    for a in range(0, width, MXU_N):
        w = min(MXU_N, width - a)
        parts.append(_mm(x2[:, a:a + w], _seg_ones(seg, w)))
    ss = parts[0] if len(parts) == 1 else jnp.concatenate(parts, axis=1)
    return lax.rsqrt(ss * (1.0 / seg) + EPS)


def _rotate_half(x, group):
    half = group // 2
    lane = lax.broadcasted_iota(jnp.int32, (1, LANES), 1)
    first = (lane & (group - 1)) < half
    from_above = pltpu.roll(x, LANES - half, 1)
    from_below = pltpu.roll(x, half, 1)
    return jnp.where(first, -from_above, from_below)


def _resident(a):
    zeros = (0,) * a.ndim
    return pl.BlockSpec(a.shape, lambda *_: zeros, pipeline_mode=pl.Buffered(1))


def _adaln_body(c_ref, w_ref, b_ref, o_ref):
    c = c_ref[...]
    a = c / (1.0 + jnp.exp(-c))
    o_ref[0, 0] = jnp.dot(a, w_ref[0], preferred_element_type=F32,
                          precision=lax.Precision.HIGHEST) + b_ref[0, 0]


def _adaln(c, ada_w, ada_b):
    L = ada_w.shape[0]
    B = c.shape[0]
    b4 = ada_b.reshape(L, 3, 1, D_MODEL)
    return pl.pallas_call(
        _adaln_body,
        out_shape=jax.ShapeDtypeStruct((L, 3, B, D_MODEL), F32),
        grid=(L, 3),
        in_specs=[
            pl.BlockSpec((B, D_MODEL), lambda l, j: (0, 0)),
            pl.BlockSpec((1, D_MODEL, D_MODEL), lambda l, j: (l, 0, j)),
            pl.BlockSpec((1, 1, 1, D_MODEL), lambda l, j: (l, j, 0, 0)),
        ],
        out_specs=pl.BlockSpec((1, 1, B, D_MODEL), lambda l, j: (l, j, 0, 0)),
        compiler_params=pltpu.CompilerParams(
            dimension_semantics=("arbitrary", "arbitrary"), vmem_limit_bytes=VMEM_LIMIT),
        name="adaln",
    )(c, ada_w, b4)


def _inproj_body(x_ref, mod_ref, gv_ref, w_ref, wuq_ref, wuk_ref, wuvt_ref, wnvt_ref, wdvt_ref,
                 cm_ref, sm_ref, cd_ref, sd_ref,
                 qm_ref, km_ref, vmt_ref, nq_ref, nk_ref, nvt_ref, dq_ref, dk_ref, dvt_ref, gt_ref):
    def gain(row, width, off=0):
        return gv_ref[row:row + 1, off:off + width]

    x = x_ref[0]
    ms = jnp.mean(x * x, axis=-1, keepdims=True)
    xn = x * lax.rsqrt(ms + EPS) * gain(G_NORM, D_MODEL)
    h = (xn * (1.0 + mod_ref[0, 1:2, :]) + mod_ref[0, 0:1, :]).astype(BF16)

    def proj(a, b):
        return _mm(h, w_ref[:, a:b])

    def lat_norm(v, row, width):
        r = lax.rsqrt(jnp.mean(v * v, axis=-1, keepdims=True) + EPS)
        return (v * r * gain(row, width)).astype(BF16)

    cm, sm = cm_ref[...], sm_ref[...]
    cd, sd = cd_ref[...], sd_ref[...]

    lat = proj(C_CQ, C_NA)
    cqn = lat_norm(lat[:, C_CQ:C_CQ + Q_LORA], G_QLAT, Q_LORA)
    qq = _mm(cqn, wuq_ref[...])
    wq_c = gain(G_MQ, LANES) * cm
    wq_s = gain(G_MQR, LANES) * sm
    q_scale = MLA_QK ** -0.5 * LOG2E
    for hd in range(MLA_HEADS):
        a = qq[:, hd * LANES:(hd + 1) * LANES]
        r = lax.rsqrt(jnp.sum(a * a, axis=-1, keepdims=True) * (1.0 / MLA_QK) + EPS)
        rotated = _rotate_half(a, MLA_ROPE)
        qm_ref[0, :, hd * LANES:(hd + 1) * LANES] = ((r * q_scale) * (a * wq_c + rotated * wq_s)).astype(BF16)

    ckvn = lat_norm(lat[:, C_CKV:C_CKV + KV_LORA], G_KVLAT, KV_LORA)
    _store_chunks(vmt_ref, _nt(wuvt_ref[...], ckvn).astype(BF16))
    kn = _mm(ckvn, wuk_ref[...])
    kpe = lat[:, C_KPE:C_KPE + LANES]
    wk_c = gain(G_MK, LANES) * cm
    k_rot = _rotate_half(kpe, MLA_ROPE) * (gain(G_MKR, LANES) * sm)
    for hd in range(MLA_HEADS):
        a = kn[:, hd * LANES:(hd + 1) * LANES] + kpe
        r = lax.rsqrt(jnp.sum(a * a, axis=-1, keepdims=True) * (1.0 / MLA_QK) + EPS)
        km_ref[0, :, hd * LANES:(hd + 1) * LANES] = (r * (a * wk_c + k_rot)).astype(BF16)

    nqk = proj(C_NA, C_NA + 2 * NA_WIDTH)
    na_scale = NA_DIM ** -0.5 * LOG2E
    a = nqk[:, :NA_WIDTH]
    nq_ref[0] = (a * (_seg_rms_scale(a, NA_DIM) * na_scale) * gain(G_NQ, NA_WIDTH)).astype(BF16)
    a = nqk[:, NA_WIDTH:]
    nk_ref[0] = (a * _seg_rms_scale(a, NA_DIM) * gain(G_NK, NA_WIDTH)).astype(BF16)
    _store_chunks(nvt_ref, _nt(wnvt_ref[...], h).astype(BF16))

    dd = proj(C_DIFF, C_DIFF + 2 * DIFF_WIDTH)
    d_scale = DIFF_QK ** -0.5 * LOG2E
    reps = DIFF_WIDTH // LANES
    cdw, sdw = jnp.concatenate([cd] * reps, axis=1), jnp.concatenate([sd] * reps, axis=1)
    for (o_ref, base, grow, grot, sc) in ((dq_ref, 0, G_DQ, G_DQR, d_scale), (dk_ref, DIFF_WIDTH, G_DK, G_DKR, 1.0)):
        a = dd[:, base:base + DIFF_WIDTH]
        b = jnp.concatenate([_rotate_half(a[:, i * LANES:(i + 1) * LANES], DIFF_QK) for i in range(reps)], axis=1)
        r = _seg_rms_scale(a, DIFF_QK)
        o_ref[0] = ((r * sc) * (a * (gain(grow, DIFF_WIDTH) * cdw) + b * (gain(grot, DIFF_WIDTH) * sdw))).astype(BF16)
    _store_chunks(dvt_ref, _nt(wdvt_ref[...], h).astype(BF16))

    g = proj(C_GATE, N_EXT)
    gt_ref[0] = (g / (1.0 + jnp.exp(-g))).astype(BF16)


def _store_chunks(ref, val):
    tk = ref.shape[3]
    for c in range(ref.shape[1]):
        ref[0, c] = val[:, c * tk:(c + 1) * tk]


def _inproj(x, mod_b, gv, w_ext, wuq, wuk, wuvt, wnvt, wdvt, tabs, tm, tk):
    B, S, _ = x.shape
    nt = S // tm
    cm, sm, cd, sd = tabs
    tok = lambda w: pl.BlockSpec((1, tm, w), lambda b, i: (b, i, 0))
    def tr(w, c):
        if c <= tm:
            return pl.BlockSpec((1, tm // c, w, c), lambda b, i: (b, i, 0, 0))
        per = c // tm
        return pl.BlockSpec((1, 1, w, tm), lambda b, i: (b, i // per, 0, i % per))
    tab = pl.BlockSpec((tm, LANES), lambda b, i: (i, 0))
    tok_out = lambda w: jax.ShapeDtypeStruct((B, S, w), BF16)
    tr_out = lambda w, c: jax.ShapeDtypeStruct((B, S // c, w, c), BF16)
    tna = NA_RB * GRID_W
    weights = (gv, w_ext, wuq, wuk, wuvt, wnvt, wdvt)
    return pl.pallas_call(
        _inproj_body,
        out_shape=[tok_out(MLA_HEADS * LANES), tok_out(MLA_HEADS * LANES), tr_out(MLA_WIDTH, tk),
                   tok_out(NA_WIDTH), tok_out(NA_WIDTH), tr_out(NA_WIDTH, tna),
                   tok_out(DIFF_WIDTH), tok_out(DIFF_WIDTH), tr_out(DIFF_WIDTH, tk), tok_out(D_MODEL)],
        grid=(B, nt),
        in_specs=[tok(D_MODEL), pl.BlockSpec((1, 3, D_MODEL), lambda b, i: (b, 0, 0)),
                  *[_resident(a) for a in weights], tab, tab, tab, tab],
        out_specs=[tok(MLA_HEADS * LANES), tok(MLA_HEADS * LANES), tr(MLA_WIDTH, tk),
                   tok(NA_WIDTH), tok(NA_WIDTH), tr(NA_WIDTH, tna),
                   tok(DIFF_WIDTH), tok(DIFF_WIDTH), tr(DIFF_WIDTH, tk), tok(D_MODEL)],
        compiler_params=pltpu.CompilerParams(
            dimension_semantics=("parallel", "parallel"), vmem_limit_bytes=VMEM_LIMIT),
        name="inproj",
    )(x, mod_b, *weights, cm, sm, cd, sd)


def _flash_chains(k_ref, vt_ref, qs, kblk, vhead, nk, m_scr, acc_scr):
    tk = vt_ref.shape[3]
    n = len(qs)

    def operands(j):
        off = j * tk if isinstance(j, int) else pl.multiple_of(j * tk, tk)
        k = k_ref[0, pl.ds(off, tk), :]
        vt = vt_ref[0, j]
        ones = jnp.ones((BF16_SUBLANES, tk), BF16)
        vts = {h: jnp.concatenate([vt[h * HEAD_V:(h + 1) * HEAD_V], ones], axis=0) for h in set(vhead)}
        return k, vts

    def scores(k, g):
        return _nt(k[:, kblk[g] * LANES:(kblk[g] + 1) * LANES], qs[g])

    def issue_ahead(k):
        pending = [scores(k, g) for g in range(min(QK_AHEAD, n))]
        for g in range(n):
            s = pending.pop(0)
            if g + QK_AHEAD < n:
                pending.append(scores(k, g + QK_AHEAD))
            yield g, s

    def exact_tile(j):
        k, vts = operands(j)
        for g, s in issue_ahead(k):
            m = m_scr[g]
            mn = jnp.maximum(m, jnp.max(s, axis=0, keepdims=True))
            p = jnp.exp2(s - mn).astype(BF16)
            acc_scr[g] = acc_scr[g] * jnp.exp2(m - mn) + _mm(vts[vhead[g]], p)
            m_scr[g] = mn

    def lazy_tiles(j0, count, first, worst):
        state = [(m_scr[g], acc_scr[g]) for g in range(n)]
        tiles = [operands(j0 + t) for t in range(count)]
        units = [(t, g) for t in range(count) for g in range(n)]
        pending = [scores(tiles[t][0], g) for t, g in units[:QK_AHEAD]]
        for i, (t, g) in enumerate(units):
            s = pending.pop(0)
            if i + QK_AHEAD < len(units):
                tn, gn = units[i + QK_AHEAD]
                pending.append(scores(tiles[tn][0], gn))
            m, acc = state[g]
            top = jnp.max(s, axis=0, keepdims=True)
            p = jnp.exp2(s - m).astype(BF16)
            mn = jnp.maximum(m, top)
            state[g] = (mn, (acc + _mm(tiles[t][1][vhead[g]], p)) * jnp.exp2(m - mn))
            off = jnp.abs(top - m) if first and t == 0 else top - m
            worst = off if worst is None else jnp.maximum(worst, off)
        for g, (m, acc) in enumerate(state):
            m_scr[g] = m
            acc_scr[g] = acc
        return worst

    def reset(m0):
        m_scr[...] = jnp.full(m_scr.shape, m0, F32)
        acc_scr[...] = jnp.zeros(acc_scr.shape, F32)

    unroll = LAZY_UNROLL if nk % LAZY_UNROLL == 0 else 1
    reset(0.0)
    worst = lazy_tiles(0, unroll, True, None)
    worst = lax.fori_loop(1, nk // unroll, lambda i, w: lazy_tiles(i * unroll, unroll, False, w), worst)

    @pl.when(jnp.max(worst) > LAZY_MAX_RISE)
    def _():
        reset(NEG)

        def step(j, c):
            exact_tile(j)
            return c

        lax.fori_loop(0, nk, step, 0)

    return [acc_scr[g] for g in range(n)]


def _normalised(acc):
    return acc[:HEAD_V] / acc[HEAD_V:HEAD_V + 1]


def _mla_body(q_ref, k_ref, vt_ref, o_ref, m_scr, acc_scr, *, nk):
    heads = list(range(MLA_HEADS))
    qs = [q_ref[0, :, h * LANES:(h + 1) * LANES] for h in heads]
    accs = _flash_chains(k_ref, vt_ref, qs, heads, heads, nk, m_scr, acc_scr)
    o_ref[0] = jnp.concatenate([_normalised(a) for a in accs], axis=0).T.astype(o_ref.dtype)


def _dense_attn_call(body, name, q, k, vt, extra, out_width, tq, chains):
    B, S, wq = q.shape
    nk, wv, tk = vt.shape[1:]
    return pl.pallas_call(
        body,
        out_shape=jax.ShapeDtypeStruct((B, S, out_width), BF16),
        grid=(B, S // tq),
        in_specs=[pl.BlockSpec((1, tq, wq), lambda b, i: (b, i, 0)),
                  pl.BlockSpec((1, S, wq), lambda b, i: (b, 0, 0)),
                  pl.BlockSpec((1, nk, wv, tk), lambda b, i: (b, 0, 0, 0)),
                  *[pl.BlockSpec(a.shape, lambda b, i: (0, 0)) for a in extra]],
        out_specs=pl.BlockSpec((1, tq, out_width), lambda b, i: (b, i, 0)),
        scratch_shapes=[pltpu.VMEM((chains, 1, tq), F32),
                        pltpu.VMEM((chains, HEAD_V + BF16_SUBLANES, tq), F32)],
        compiler_params=pltpu.CompilerParams(
            dimension_semantics=("parallel", "arbitrary"), vmem_limit_bytes=VMEM_LIMIT),
        name=name,
    )(q, k, vt, *extra)


def _mla_attn(q, k, vt, tq):
    return _dense_attn_call(functools.partial(_mla_body, nk=vt.shape[1]), "mla_attn", q, k, vt, (), MLA_WIDTH, tq,
                            MLA_HEADS)


def _diff_body(q_ref, k_ref, vt_ref, lam_ref, sg_ref, o_ref, m_scr, acc_scr, *, nk, lam_init):
    lane = lax.broadcasted_iota(jnp.int32, (1, LANES), 1)
    per_blk = LANES // DIFF_QK
    chains = list(range(2 * DIFF_HEADS))
    qs = []
    for g in chains:
        qb = q_ref[0, :, (g // per_blk) * LANES:(g // per_blk + 1) * LANES]
        lo = (g % per_blk) * DIFF_QK
        qs.append(jnp.where((lane >= lo) & (lane < lo + DIFF_QK), qb, jnp.zeros_like(qb)))
    accs = _flash_chains(k_ref, vt_ref, qs, [g // per_blk for g in chains], [g // 2 for g in chains], nk,
                         m_scr, acc_scr)

    lv = lam_ref[...]
    lam = (jnp.exp(jnp.sum(lv[0:1] * lv[1:2], axis=-1, keepdims=True))
           - jnp.exp(jnp.sum(lv[2:3] * lv[3:4], axis=-1, keepdims=True)) + lam_init)
    outs = []
    for h in range(DIFF_HEADS):
        o = _normalised(accs[2 * h]) - lam * _normalised(accs[2 * h + 1])
        r = lax.rsqrt(jnp.sum(o * o, axis=0, keepdims=True) * (1.0 / DIFF_V) + EPS)
        outs.append(o * r * (sg_ref[...] * (1.0 - lam_init)))
    o_ref[0] = jnp.concatenate(outs, axis=0).T.astype(o_ref.dtype)


def _diff_attn(q, k, vt, lamv, sgt, lam_init, tq):
    body = functools.partial(_diff_body, nk=vt.shape[1], lam_init=lam_init)
    return _dense_attn_call(body, "diff_attn", q, k, vt, (lamv, sgt), DIFF_WIDTH, tq, 2 * DIFF_HEADS)


def _na_body(q_ref, k_ref, vt_ref, bm_ref, o_ref, *, nrb):
    tq = NA_RB * GRID_W
    nkeys = NA_KROWS * GRID_W
    ones = jnp.ones((BF16_SUBLANES, nkeys), BF16)
    lane = lax.broadcasted_iota(jnp.int32, (1, LANES), 1)
    per_blk = LANES // NA_DIM

    def attend(b, shifted):
        rb = pl.program_id(1) * NA_STEP_BLOCKS + b
        variant = jnp.where(rb == 0, 0, jnp.where(rb == nrb - 1, 2, 1))
        chunk0 = jnp.clip(rb - 1, 0, nrb - 3)
        k = k_ref[0, pl.ds(pl.multiple_of(chunk0 * tq, tq), nkeys), :]
        vt = jnp.concatenate([vt_ref[0, chunk0 + c] for c in range(NA_KROWS // NA_RB)], axis=1)

        def scores(h):
            blk = slice((h // per_blk) * LANES, (h // per_blk + 1) * LANES)
            qb = q_ref[0, b * tq:(b + 1) * tq, blk]
            lo = (h % per_blk) * NA_DIM
            qh = jnp.where((lane >= lo) & (lane < lo + NA_DIM), qb, jnp.zeros_like(qb))
            return _nt(k[:, blk], qh) + bm_ref[h, variant]

        pending = [scores(h) for h in range(min(QK_AHEAD, NA_HEADS))]
        outs, worst = [], None
        for h in range(NA_HEADS):
            s = pending.pop(0)
            if h + QK_AHEAD < NA_HEADS:
                pending.append(scores(h + QK_AHEAD))
            top = jnp.max(s, axis=0, keepdims=True)
            p = jnp.exp2(s - top if shifted else s).astype(BF16)
            vt_ext = jnp.concatenate([vt[h * NA_DIM:(h + 1) * NA_DIM], ones], axis=0)
            acc = _mm(vt_ext, p)
            outs.append(acc[:NA_DIM] / acc[NA_DIM:NA_DIM + 1])
            worst = jnp.abs(top) if worst is None else jnp.maximum(worst, jnp.abs(top))
        o_ref[0, b * tq:(b + 1) * tq] = jnp.concatenate(outs, axis=0).T.astype(o_ref.dtype)
        return worst

    def all_blocks(shifted):
        return functools.reduce(jnp.maximum, [attend(b, shifted) for b in range(NA_STEP_BLOCKS)])

    worst = all_blocks(shifted=False)

    @pl.when(jnp.max(worst) > LAZY_MAX_RISE)
    def _():
        all_blocks(shifted=True)


def _na_attn(q, k, vt, bm):
    B, S, _ = q.shape
    tq = NA_RB * GRID_W
    nrb = S // tq
    step = NA_STEP_BLOCKS * tq
    return pl.pallas_call(
        functools.partial(_na_body, nrb=nrb),
        out_shape=jax.ShapeDtypeStruct((B, S, NA_WIDTH), BF16),
        grid=(B, S // step),
        in_specs=[pl.BlockSpec((1, step, NA_WIDTH), lambda b, i: (b, i, 0)),
                  pl.BlockSpec((1, S, NA_WIDTH), lambda b, i: (b, 0, 0)),
                  pl.BlockSpec((1, nrb, NA_WIDTH, tq), lambda b, i: (b, 0, 0, 0)),
                  _resident(bm)],
        out_specs=pl.BlockSpec((1, step, NA_WIDTH), lambda b, i: (b, i, 0)),
        compiler_params=pltpu.CompilerParams(
            dimension_semantics=("parallel", "arbitrary"), vmem_limit_bytes=VMEM_LIMIT),
        name="na_attn",
    )(q, k, vt, bm)


def _na_bias_table(rpb, rows):
    nrb = rows // NA_RB
    W = GRID_W
    H = rpb.shape[0]
    pad_c = W - NA_KC
    rp = jnp.pad(jnp.flip(rpb.astype(F32), axis=-1), ((0, 0), (0, 0), (pad_c, pad_c + 1)))
    skew = jnp.tile(rp, (1, 1, W))[:, :, :W * (2 * W - 1)].reshape(H, -1, W, 2 * W - 1)
    pad_lo, pad_hi = NA_KR // 2, NA_KROWS - NA_KR
    t = jnp.pad(skew[:, :, :, W - 1:] * LOG2E, ((0, 0), (pad_lo, pad_hi), (0, 0), (0, 0)))
    kri = np.arange(NA_KROWS)[:, None, None]
    kc = np.arange(W)[None, :, None]
    c = np.arange(W)[None, None, :]
    c0 = np.clip(c - NA_KC // 2, 0, W - NA_KC)
    col_ok = (kc >= c0) & (kc < c0 + NA_KC)
    variants = []
    for rb in (0, 1, nrb - 1):
        start = NA_RB * min(max(rb - 1, 0), nrb - 3)
        per_row = []
        for ri in range(NA_RB):
            r = NA_RB * rb + ri
            r0 = min(max(r - NA_KR // 2, 0), rows - NA_KR)
            d0 = start - r + (NA_KR - 1) + pad_lo
            assert 0 <= d0 and d0 + NA_KROWS <= t.shape[1]
            ok = ((start + kri >= r0) & (start + kri < r0 + NA_KR)) & col_ok
            per_row.append(jnp.where(ok[None], t[:, d0:d0 + NA_KROWS], NEG))
        variants.append(jnp.concatenate(per_row, axis=-1))
    return jnp.stack(variants, axis=1).reshape(H, 3, NA_KROWS * W, NA_RB * W)


def _outproj_body(x_ref, om_ref, on_ref, od_ref, gt_ref, gate_ref, w_ref, o_ref):
    def branch(o_ref_, a, b):
        mix = (o_ref_[0].astype(F32) * gt_ref[0, :, a:b].astype(F32)).astype(BF16)
        return _mm(mix, w_ref[a:b, :])

    y = (branch(om_ref, 0, MLA_WIDTH) + branch(on_ref, MLA_WIDTH, MLA_WIDTH + NA_WIDTH)
         + branch(od_ref, MLA_WIDTH + NA_WIDTH, D_MODEL))
    o_ref[0] = x_ref[0] + gate_ref[0, 2:3, :] * y


def _outproj(x, om, on, od, gt, mod_b, w_out, tm):
    B, S, _ = x.shape
    tok = lambda w: pl.BlockSpec((1, tm, w), lambda b, i: (b, i, 0))
    return pl.pallas_call(
        _outproj_body,
        out_shape=jax.ShapeDtypeStruct((B, S, D_MODEL), F32),
        grid=(B, S // tm),
        in_specs=[tok(D_MODEL), tok(MLA_WIDTH), tok(NA_WIDTH), tok(DIFF_WIDTH), tok(D_MODEL),
                  pl.BlockSpec((1, 3, D_MODEL), lambda b, i: (b, 0, 0)),
                  _resident(w_out)],
        out_specs=tok(D_MODEL),
        compiler_params=pltpu.CompilerParams(
            dimension_semantics=("parallel", "parallel"), vmem_limit_bytes=VMEM_LIMIT),
        name="outproj",
    )(x, om, on, od, gt, mod_b, w_out)


def _swap_halves(g, half):
    return jnp.concatenate([g[..., half:], g[..., :half]], axis=-1)


def _pad_cols(w, left, total):
    return jnp.pad(w, ((0, 0), (left, total - left - w.shape[-1])))


def _prep_layer(w_in, w_uq, w_ukv, norm_g, q_lat_g, kv_lat_g, mla_q_g, mla_k_g,
                na_q_g, na_k_g, diff_q_g, diff_k_g):
    sizes = (Q_LORA, KV_LORA, MLA_ROPE, 3 * NA_WIDTH, 3 * DIFF_WIDTH, D_MODEL)
    splits = np.cumsum(sizes)[:-1].tolist()
    w_cq, w_ckv, w_kpe, w_na, w_diff, w_gate = jnp.split(w_in, splits, axis=-1)
    hr = MLA_ROPE // 2
    hd = DIFF_QK // 2
    w_qd, w_kd, w_vd = jnp.split(w_diff, 3, axis=-1)
    w_nq, w_nk, w_nv = jnp.split(w_na, 3, axis=-1)
    w_ext = jnp.concatenate([
        w_cq, w_ckv,
        _pad_cols(w_kpe, MLA_NOPE, LANES), w_nq, w_nk, w_qd, w_kd, w_gate], axis=-1).astype(BF16)
    wnvt = w_nv.T.astype(BF16)
    wdvt = w_vd.T.astype(BF16)

    uq = w_uq.reshape(Q_LORA, MLA_HEADS, MLA_QK)
    wuq = jnp.pad(uq, ((0, 0), (0, 0), (0, LANES - MLA_QK))).reshape(Q_LORA, -1).astype(BF16)

    ukv = w_ukv.reshape(KV_LORA, MLA_HEADS, MLA_NOPE + MLA_V)
    wuk = jnp.pad(ukv[..., :MLA_NOPE], ((0, 0), (0, 0), (0, LANES - MLA_NOPE))).reshape(KV_LORA, -1).astype(BF16)
    wuvt = ukv[..., MLA_NOPE:].reshape(KV_LORA, -1).T.astype(BF16)

    def row(v):
        return jnp.pad(v.astype(F32), (0, D_MODEL - v.shape[0]))

    def mla_rows(g):
        rot = jnp.pad(_swap_halves(g[MLA_NOPE:], hr), (MLA_NOPE, 0))
        return row(g), row(rot)

    def diff_rows(g):
        n = DIFF_WIDTH // DIFF_QK
        return row(jnp.tile(g, n)), row(jnp.tile(_swap_halves(g, hd), n))

    rows = [row(norm_g), row(q_lat_g), row(kv_lat_g), *mla_rows(mla_q_g), *mla_rows(mla_k_g),
            row(jnp.tile(na_q_g, NA_HEADS)), row(jnp.tile(na_k_g, NA_HEADS)),
            *diff_rows(diff_q_g), *diff_rows(diff_k_g)]
    rows += [jnp.zeros((D_MODEL,), F32)] * (G_ROWS - len(rows))
    return w_ext, wuq, wuk, wuvt, wnvt, wdvt, jnp.stack(rows)


def _rope_tables(S):
    def cs(dim):
        inv = ROPE_THETA ** (-jnp.arange(0, dim, 2, dtype=F32) / dim)
        ang = jnp.arange(S, dtype=F32)[:, None] * inv[None, :]
        return jnp.cos(ang), jnp.sin(ang)

    c, s = cs(MLA_ROPE)
    pad = LANES - MLA_QK
    cm = jnp.concatenate([jnp.ones((S, MLA_NOPE), F32), c, c, jnp.zeros((S, pad), F32)], axis=-1)
    sm = jnp.concatenate([jnp.zeros((S, MLA_NOPE), F32), s, s, jnp.zeros((S, pad), F32)], axis=-1)
    c, s = cs(DIFF_QK)
    n = LANES // DIFF_QK
    cd = jnp.tile(jnp.concatenate([c, c], axis=-1), (1, n))
    sd = jnp.tile(jnp.concatenate([s, s], axis=-1), (1, n))
    return cm, sm, cd, sd


def _tile(S, pref):
    t = pref
    while S % t:
        t //= 2
    return t


def kernel(x, c, ada_w, ada_b, norm_g, w_in, q_lat_g, w_uq, kv_lat_g, w_ukv, mla_q_g, mla_k_g,
           na_q_g, na_k_g, na_rpb, diff_q_g, diff_k_g, lam_q1, lam_k1, lam_q2, lam_k2, subln_g, w_out):
    B, S, D = x.shape
    L = ada_w.shape[0]
    rows = S // GRID_W
    assert D == D_MODEL and S % (NA_STEP_BLOCKS * NA_RB * GRID_W) == 0 and rows >= NA_KROWS

    tabs = _rope_tables(S)
    mod = _adaln(c, ada_w, ada_b)
    tm = _tile(S, TILE_M)
    tk = _tile(S, TILE_K)
    tqd = _tile(S, TILE_Q_DIFF)

    weights = jax.vmap(_prep_layer)(w_in, w_uq, w_ukv, norm_g, q_lat_g, kv_lat_g, mla_q_g, mla_k_g,
                                    na_q_g, na_k_g, diff_q_g, diff_k_g)
    bias = jax.vmap(functools.partial(_na_bias_table, rows=rows))(na_rpb)
    mod_b = mod.transpose(0, 2, 1, 3)
    lamv = jnp.stack([lam_q1, lam_k1, lam_q2, lam_k2], axis=1).astype(F32)
    sgt = jnp.broadcast_to(subln_g.astype(F32)[:, :, None], (L, DIFF_V, tqd))
    w_out_b = w_out.astype(BF16)

    h = x
    for i in range(L):
        lam_init = 0.8 - 0.6 * math.exp(-0.3 * i)
        w_ext, wuq, wuk, wuvt, wnvt, wdvt, gv = (w[i] for w in weights)
        qm, km, vmt, nq, nk, nvt, dq, dk, dvt, gt = _inproj(h, mod_b[i], gv, w_ext, wuq, wuk, wuvt, wnvt, wdvt,
                                                            tabs, tm, tk)
        om = _mla_attn(qm, km, vmt, _tile(S, TILE_Q_MLA))
        on = _na_attn(nq, nk, nvt, bias[i])
        od = _diff_attn(dq, dk, dvt, lamv[i], sgt[i], lam_init, tqd)
        h = _outproj(h, om, on, od, gt, mod_b[i], w_out_b[i], tm)
    return h
```

```python
import functools
import math

import numpy as np
import jax
import jax.numpy as jnp
from jax import lax
from jax.experimental import pallas as pl
from jax.experimental.pallas import tpu as pltpu

F32 = jnp.float32
BF16 = jnp.bfloat16

D_MODEL = 1024
GRID_W = 64
EPS = 1e-6
ROPE_THETA = 10000.0
MLA_HEADS, MLA_NOPE, MLA_ROPE, MLA_V = 6, 64, 32, 64
MLA_QK = MLA_NOPE + MLA_ROPE
Q_LORA, KV_LORA = 256, 128
NA_HEADS, NA_DIM, NA_KR, NA_KC = 6, 64, 8, 16
DIFF_HEADS, DIFF_QK = 4, 32
DIFF_V = 2 * DIFF_QK
MLA_WIDTH, NA_WIDTH, DIFF_WIDTH = MLA_HEADS * MLA_V, NA_HEADS * NA_DIM, DIFF_HEADS * DIFF_V
HEAD_V = 64

LANES = 128
BF16_SUBLANES = 16
MXU_N = 256
VMEM_LIMIT = 52 * 1024 * 1024

C_CQ, C_CKV, C_KPE = 0, 256, 384
C_NA = 512
C_DIFF = 1280
C_GATE = 1792
N_EXT = 2816

G_NORM, G_QLAT, G_KVLAT, G_MQ, G_MQR, G_MK, G_MKR, G_NQ, G_NK, G_DQ, G_DQR, G_DK, G_DKR = range(13)
G_ROWS = 16

TILE_M = 1024
TILE_Q_MLA = 1024
TILE_Q_DIFF = 1024
TILE_K = 512
LAZY_UNROLL = 1
QK_AHEAD = 5

NA_RB = 4
NA_STEP_BLOCKS = 4
NA_KROWS = 12
NEG = -1e30
LOG2E = math.log2(math.e)
LAZY_MAX_RISE = 64.0


def _nt(a, b):
    return lax.dot_general(a, b, (((1,), (1,)), ((), ())), preferred_element_type=F32)


def _mm(a, b):
    return jnp.dot(a, b, preferred_element_type=F32)


def _seg_ones(seg, width):
    sh = int(math.log2(seg))
    r = lax.broadcasted_iota(jnp.int32, (width, width), 0) >> sh
    c = lax.broadcasted_iota(jnp.int32, (width, width), 1) >> sh
    return jnp.where(r == c, 1.0, 0.0).astype(BF16)


def _seg_rms_scale(x, seg):
    x2 = (x * x).astype(BF16)
    width = x.shape[1]
    parts = []
    for a in range(0, width, MXU_N):
        w = min(MXU_N, width - a)
        parts.append(_mm(x2[:, a:a + w], _seg_ones(seg, w)))
    ss = parts[0] if len(parts) == 1 else jnp.concatenate(parts, axis=1)
    return lax.rsqrt(ss * (1.0 / seg) + EPS)


def _rotate_half(x, group):
    half = group // 2
    lane = lax.broadcasted_iota(jnp.int32, (1, LANES), 1)
    first = (lane & (group - 1)) < half
    from_above = pltpu.roll(x, LANES - half, 1)
    from_below = pltpu.roll(x, half, 1)
    return jnp.where(first, -from_above, from_below)


def _resident(a):
    zeros = (0,) * a.ndim
    return pl.BlockSpec(a.shape, lambda *_: zeros, pipeline_mode=pl.Buffered(1))


def _adaln_body(c_ref, w_ref, b_ref, o_ref):
    c = c_ref[...]
    a = c / (1.0 + jnp.exp(-c))
    o_ref[0, 0] = jnp.dot(a, w_ref[0], preferred_element_type=F32,
                          precision=lax.Precision.HIGHEST) + b_ref[0, 0]


def _adaln(c, ada_w, ada_b):
    L = ada_w.shape[0]
    B = c.shape[0]
    b4 = ada_b.reshape(L, 3, 1, D_MODEL)
    return pl.pallas_call(
        _adaln_body,
        out_shape=jax.ShapeDtypeStruct((L, 3, B, D_MODEL), F32),
        grid=(L, 3),
        in_specs=[
            pl.BlockSpec((B, D_MODEL), lambda l, j: (0, 0)),
            pl.BlockSpec((1, D_MODEL, D_MODEL), lambda l, j: (l, 0, j)),
            pl.BlockSpec((1, 1, 1, D_MODEL), lambda l, j: (l, j, 0, 0)),
        ],
        out_specs=pl.BlockSpec((1, 1, B, D_MODEL), lambda l, j: (l, j, 0, 0)),
        compiler_params=pltpu.CompilerParams(
            dimension_semantics=("arbitrary", "arbitrary"), vmem_limit_bytes=VMEM_LIMIT),
        name="adaln",
    )(c, ada_w, b4)


def _inproj_body(x_ref, mod_ref, gv_ref, w_ref, wuq_ref, wuk_ref, wuvt_ref, wnvt_ref, wdvt_ref,
                 cm_ref, sm_ref, cd_ref, sd_ref,
                 qm_ref, km_ref, vmt_ref, nq_ref, nk_ref, nvt_ref, dq_ref, dk_ref, dvt_ref, gt_ref):
    def gain(row, width, off=0):
        return gv_ref[row:row + 1, off:off + width]

    x = x_ref[0]
    ms = jnp.mean(x * x, axis=-1, keepdims=True)
    xn = x * lax.rsqrt(ms + EPS) * gain(G_NORM, D_MODEL)
    h = (xn * (1.0 + mod_ref[0, 1:2, :]) + mod_ref[0, 0:1, :]).astype(BF16)

    def proj(a, b):
        return _mm(h, w_ref[:, a:b])

    def lat_norm(v, row, width):
        r = lax.rsqrt(jnp.mean(v * v, axis=-1, keepdims=True) + EPS)
        return (v * r * gain(row, width)).astype(BF16)

    cm, sm = cm_ref[...], sm_ref[...]
    cd, sd = cd_ref[...], sd_ref[...]

    lat = proj(C_CQ, C_NA)
    cqn = lat_norm(lat[:, C_CQ:C_CQ + Q_LORA], G_QLAT, Q_LORA)
    qq = _mm(cqn, wuq_ref[...])
    wq_c = gain(G_MQ, LANES) * cm
    wq_s = gain(G_MQR, LANES) * sm
    q_scale = MLA_QK ** -0.5 * LOG2E
    for hd in range(MLA_HEADS):
        a = qq[:, hd * LANES:(hd + 1) * LANES]
        r = lax.rsqrt(jnp.sum(a * a, axis=-1, keepdims=True) * (1.0 / MLA_QK) + EPS)
        rotated = _rotate_half(a, MLA_ROPE)
        qm_ref[0, :, hd * LANES:(hd + 1) * LANES] = ((r * q_scale) * (a * wq_c + rotated * wq_s)).astype(BF16)

    ckvn = lat_norm(lat[:, C_CKV:C_CKV + KV_LORA], G_KVLAT, KV_LORA)
    _store_chunks(vmt_ref, _nt(wuvt_ref[...], ckvn).astype(BF16))
    kn = _mm(ckvn, wuk_ref[...])
    kpe = lat[:, C_KPE:C_KPE + LANES]
    wk_c = gain(G_MK, LANES) * cm
    k_rot = _rotate_half(kpe, MLA_ROPE) * (gain(G_MKR, LANES) * sm)
    for hd in range(MLA_HEADS):
        a = kn[:, hd * LANES:(hd + 1) * LANES] + kpe
        r = lax.rsqrt(jnp.sum(a * a, axis=-1, keepdims=True) * (1.0 / MLA_QK) + EPS)
        km_ref[0, :, hd * LANES:(hd + 1) * LANES] = (r * (a * wk_c + k_rot)).astype(BF16)

    nqk = proj(C_NA, C_NA + 2 * NA_WIDTH)
    na_scale = NA_DIM ** -0.5 * LOG2E
    a = nqk[:, :NA_WIDTH]
    nq_ref[0] = (a * (_seg_rms_scale(a, NA_DIM) * na_scale) * gain(G_NQ, NA_WIDTH)).astype(BF16)
    a = nqk[:, NA_WIDTH:]
    nk_ref[0] = (a * _seg_rms_scale(a, NA_DIM) * gain(G_NK, NA_WIDTH)).astype(BF16)
    _store_chunks(nvt_ref, _nt(wnvt_ref[...], h).astype(BF16))

    dd = proj(C_DIFF, C_DIFF + 2 * DIFF_WIDTH)
    d_scale = DIFF_QK ** -0.5 * LOG2E
    reps = DIFF_WIDTH // LANES
    cdw, sdw = jnp.concatenate([cd] * reps, axis=1), jnp.concatenate([sd] * reps, axis=1)
    for (o_ref, base, grow, grot, sc) in ((dq_ref, 0, G_DQ, G_DQR, d_scale), (dk_ref, DIFF_WIDTH, G_DK, G_DKR, 1.0)):
        a = dd[:, base:base + DIFF_WIDTH]
        b = jnp.concatenate([_rotate_half(a[:, i * LANES:(i + 1) * LANES], DIFF_QK) for i in range(reps)], axis=1)
        r = _seg_rms_scale(a, DIFF_QK)
        o_ref[0] = ((r * sc) * (a * (gain(grow, DIFF_WIDTH) * cdw) + b * (gain(grot, DIFF_WIDTH) * sdw))).astype(BF16)
    _store_chunks(dvt_ref, _nt(wdvt_ref[...], h).astype(BF16))

    g = proj(C_GATE, N_EXT)
    gt_ref[0] = (g / (1.0 + jnp.exp(-g))).astype(BF16)


def _store_chunks(ref, val):
    tk = ref.shape[3]
    for c in range(ref.shape[1]):
        ref[0, c] = val[:, c * tk:(c + 1) * tk]


def _inproj(x, mod_b, gv, w_ext, wuq, wuk, wuvt, wnvt, wdvt, tabs, tm, tk):
    B, S, _ = x.shape
    nt = S // tm
    cm, sm, cd, sd = tabs
    tok = lambda w: pl.BlockSpec((1, tm, w), lambda b, i: (b, i, 0))
    def tr(w, c):
        if c <= tm:
            return pl.BlockSpec((1, tm // c, w, c), lambda b, i: (b, i, 0, 0))
        per = c // tm
        return pl.BlockSpec((1, 1, w, tm), lambda b, i: (b, i // per, 0, i % per))
    tab = pl.BlockSpec((tm, LANES), lambda b, i: (i, 0))
    tok_out = lambda w: jax.ShapeDtypeStruct((B, S, w), BF16)
    tr_out = lambda w, c: jax.ShapeDtypeStruct((B, S // c, w, c), BF16)
    tna = NA_RB * GRID_W
    weights = (gv, w_ext, wuq, wuk, wuvt, wnvt, wdvt)
    return pl.pallas_call(
        _inproj_body,
        out_shape=[tok_out(MLA_HEADS * LANES), tok_out(MLA_HEADS * LANES), tr_out(MLA_WIDTH, tk),
                   tok_out(NA_WIDTH), tok_out(NA_WIDTH), tr_out(NA_WIDTH, tna),
                   tok_out(DIFF_WIDTH), tok_out(DIFF_WIDTH), tr_out(DIFF_WIDTH, tk), tok_out(D_MODEL)],
        grid=(B, nt),
        in_specs=[tok(D_MODEL), pl.BlockSpec((1, 3, D_MODEL), lambda b, i: (b, 0, 0)),
                  *[_resident(a) for a in weights], tab, tab, tab, tab],
        out_specs=[tok(MLA_HEADS * LANES), tok(MLA_HEADS * LANES), tr(MLA_WIDTH, tk),
                   tok(NA_WIDTH), tok(NA_WIDTH), tr(NA_WIDTH, tna),
                   tok(DIFF_WIDTH), tok(DIFF_WIDTH), tr(DIFF_WIDTH, tk), tok(D_MODEL)],
        compiler_params=pltpu.CompilerParams(
            dimension_semantics=("parallel", "parallel"), vmem_limit_bytes=VMEM_LIMIT),
        name="inproj",
    )(x, mod_b, *weights, cm, sm, cd, sd)


def _flash_chains(k_ref, vt_ref, qs, kblk, vhead, nk, m_scr, acc_scr):
    tk = vt_ref.shape[3]
    n = len(qs)

    def operands(j):
        off = j * tk if isinstance(j, int) else pl.multiple_of(j * tk, tk)
        ones = jnp.ones((BF16_SUBLANES, tk), BF16)
        vts = {h: jnp.concatenate([vt_ref[0, j, h * HEAD_V:(h + 1) * HEAD_V, :], ones], axis=0) for h in set(vhead)}
        return off, vts

    def scores(off, g):
        return _nt(k_ref[0, pl.ds(off, tk), kblk[g] * LANES:(kblk[g] + 1) * LANES], qs[g])

    def issue_ahead(k):
        pending = [scores(k, g) for g in range(min(QK_AHEAD, n))]
        for g in range(n):
            s = pending.pop(0)
            if g + QK_AHEAD < n:
                pending.append(scores(k, g + QK_AHEAD))
            yield g, s

    def exact_tile(j):
        k, vts = operands(j)
        for g, s in issue_ahead(k):
            m = m_scr[g]
            mn = jnp.maximum(m, jnp.max(s, axis=0, keepdims=True))
            p = jnp.exp2(s - mn).astype(BF16)
            acc_scr[g] = acc_scr[g] * jnp.exp2(m - mn) + _mm(vts[vhead[g]], p)
            m_scr[g] = mn

    def lazy_tiles(j0, count, first, worst):
        state = [(m_scr[g], acc_scr[g]) for g in range(n)]
        tiles = [operands(j0 + t) for t in range(count)]
        units = [(t, g) for t in range(count) for g in range(n)]
        pending = [scores(tiles[t][0], g) for t, g in units[:QK_AHEAD]]
        for i, (t, g) in enumerate(units):
            s = pending.pop(0)
            if i + QK_AHEAD < len(units):
                tn, gn = units[i + QK_AHEAD]
                pending.append(scores(tiles[tn][0], gn))
            m, acc = state[g]
            top = jnp.max(s, axis=0, keepdims=True)
            p = jnp.exp2(s - m).astype(BF16)
            mn = jnp.maximum(m, top)
            state[g] = (mn, (acc + _mm(tiles[t][1][vhead[g]], p)) * jnp.exp2(m - mn))
            off = jnp.abs(top - m) if first and t == 0 else top - m
            worst = off if worst is None else jnp.maximum(worst, off)
        for g, (m, acc) in enumerate(state):
            m_scr[g] = m
            acc_scr[g] = acc
        return worst

    def reset(m0):
        m_scr[...] = jnp.full(m_scr.shape, m0, F32)
        acc_scr[...] = jnp.zeros(acc_scr.shape, F32)

    unroll = LAZY_UNROLL if nk % LAZY_UNROLL == 0 else 1
    reset(0.0)
    worst = lazy_tiles(0, unroll, True, None)
    worst = lax.fori_loop(1, nk // unroll, lambda i, w: lazy_tiles(i * unroll, unroll, False, w), worst)

    @pl.when(jnp.max(worst) > LAZY_MAX_RISE)
    def _():
        reset(NEG)

        def step(j, c):
            exact_tile(j)
            return c

        lax.fori_loop(0, nk, step, 0)

    return [acc_scr[g] for g in range(n)]


def _normalised(acc):
    return acc[:HEAD_V] / acc[HEAD_V:HEAD_V + 1]


def _mla_body(q_ref, k_ref, vt_ref, o_ref, m_scr, acc_scr, *, nk):
    heads = list(range(MLA_HEADS))
    qs = [q_ref[0, :, h * LANES:(h + 1) * LANES] for h in heads]
    accs = _flash_chains(k_ref, vt_ref, qs, heads, heads, nk, m_scr, acc_scr)
    o_ref[0] = jnp.concatenate([_normalised(a) for a in accs], axis=0).T.astype(o_ref.dtype)


def _dense_attn_call(body, name, q, k, vt, extra, out_width, tq, chains):
    B, S, wq = q.shape
    nk, wv, tk = vt.shape[1:]
    return pl.pallas_call(
        body,
        out_shape=jax.ShapeDtypeStruct((B, S, out_width), BF16),
        grid=(B, S // tq),
        in_specs=[pl.BlockSpec((1, tq, wq), lambda b, i: (b, i, 0)),
                  pl.BlockSpec((1, S, wq), lambda b, i: (b, 0, 0)),
                  pl.BlockSpec((1, nk, wv, tk), lambda b, i: (b, 0, 0, 0)),
                  *[pl.BlockSpec(a.shape, lambda b, i: (0, 0)) for a in extra]],
        out_specs=pl.BlockSpec((1, tq, out_width), lambda b, i: (b, i, 0)),
        scratch_shapes=[pltpu.VMEM((chains, 1, tq), F32),
                        pltpu.VMEM((chains, HEAD_V + BF16_SUBLANES, tq), F32)],
        compiler_params=pltpu.CompilerParams(
            dimension_semantics=("parallel", "arbitrary"), vmem_limit_bytes=VMEM_LIMIT),
        name=name,
    )(q, k, vt, *extra)


def _mla_attn(q, k, vt, tq):
    return _dense_attn_call(functools.partial(_mla_body, nk=vt.shape[1]), "mla_attn", q, k, vt, (), MLA_WIDTH, tq,
                            MLA_HEADS)


def _diff_body(q_ref, k_ref, vt_ref, lam_ref, sg_ref, o_ref, m_scr, acc_scr, *, nk, lam_init):
    lane = lax.broadcasted_iota(jnp.int32, (1, LANES), 1)
    per_blk = LANES // DIFF_QK
    chains = list(range(2 * DIFF_HEADS))
    qs = []
    for g in chains:
        qb = q_ref[0, :, (g // per_blk) * LANES:(g // per_blk + 1) * LANES]
        lo = (g % per_blk) * DIFF_QK
        qs.append(jnp.where((lane >= lo) & (lane < lo + DIFF_QK), qb, jnp.zeros_like(qb)))
    accs = _flash_chains(k_ref, vt_ref, qs, [g // per_blk for g in chains], [g // 2 for g in chains], nk,
                         m_scr, acc_scr)

    lv = lam_ref[...]
    lam = (jnp.exp(jnp.sum(lv[0:1] * lv[1:2], axis=-1, keepdims=True))
           - jnp.exp(jnp.sum(lv[2:3] * lv[3:4], axis=-1, keepdims=True)) + lam_init)
    outs = []
    for h in range(DIFF_HEADS):
        o = _normalised(accs[2 * h]) - lam * _normalised(accs[2 * h + 1])
        r = lax.rsqrt(jnp.sum(o * o, axis=0, keepdims=True) * (1.0 / DIFF_V) + EPS)
        outs.append(o * r * (sg_ref[...] * (1.0 - lam_init)))
    o_ref[0] = jnp.concatenate(outs, axis=0).T.astype(o_ref.dtype)


def _diff_attn(q, k, vt, lamv, sgt, lam_init, tq):
    body = functools.partial(_diff_body, nk=vt.shape[1], lam_init=lam_init)
    return _dense_attn_call(body, "diff_attn", q, k, vt, (lamv, sgt), DIFF_WIDTH, tq, 2 * DIFF_HEADS)


def _na_body(q_ref, k_ref, vt_ref, bm_ref, o_ref, *, nrb):
    tq = NA_RB * GRID_W
    nkeys = NA_KROWS * GRID_W
    ones = jnp.ones((BF16_SUBLANES, nkeys), BF16)
    lane = lax.broadcasted_iota(jnp.int32, (1, LANES), 1)
    per_blk = LANES // NA_DIM

    def attend(b, shifted):
        rb = pl.program_id(1) * NA_STEP_BLOCKS + b
        variant = jnp.where(rb == 0, 0, jnp.where(rb == nrb - 1, 2, 1))
        chunk0 = jnp.clip(rb - 1, 0, nrb - 3)
        k = k_ref[0, pl.ds(pl.multiple_of(chunk0 * tq, tq), nkeys), :]
        vt = jnp.concatenate([vt_ref[0, chunk0 + c] for c in range(NA_KROWS // NA_RB)], axis=1)

        def scores(h):
            blk = slice((h // per_blk) * LANES, (h // per_blk + 1) * LANES)
            qb = q_ref[0, b * tq:(b + 1) * tq, blk]
            lo = (h % per_blk) * NA_DIM
            qh = jnp.where((lane >= lo) & (lane < lo + NA_DIM), qb, jnp.zeros_like(qb))
            return _nt(k[:, blk], qh) + bm_ref[h, variant]

        pending = [scores(h) for h in range(min(QK_AHEAD, NA_HEADS))]
        outs, worst = [], None
        for h in range(NA_HEADS):
            s = pending.pop(0)
            if h + QK_AHEAD < NA_HEADS:
                pending.append(scores(h + QK_AHEAD))
            top = jnp.max(s, axis=0, keepdims=True)
            p = jnp.exp2(s - top if shifted else s).astype(BF16)
            vt_ext = jnp.concatenate([vt[h * NA_DIM:(h + 1) * NA_DIM], ones], axis=0)
            acc = _mm(vt_ext, p)
            outs.append(acc[:NA_DIM] / acc[NA_DIM:NA_DIM + 1])
            worst = jnp.abs(top) if worst is None else jnp.maximum(worst, jnp.abs(top))
        o_ref[0, b * tq:(b + 1) * tq] = jnp.concatenate(outs, axis=0).T.astype(o_ref.dtype)
        return worst

    def all_blocks(shifted):
        return functools.reduce(jnp.maximum, [attend(b, shifted) for b in range(NA_STEP_BLOCKS)])

    worst = all_blocks(shifted=False)

    @pl.when(jnp.max(worst) > LAZY_MAX_RISE)
    def _():
        all_blocks(shifted=True)


def _na_attn(q, k, vt, bm):
    B, S, _ = q.shape
    tq = NA_RB * GRID_W
    nrb = S // tq
    step = NA_STEP_BLOCKS * tq
    return pl.pallas_call(
        functools.partial(_na_body, nrb=nrb),
        out_shape=jax.ShapeDtypeStruct((B, S, NA_WIDTH), BF16),
        grid=(B, S // step),
        in_specs=[pl.BlockSpec((1, step, NA_WIDTH), lambda b, i: (b, i, 0)),
                  pl.BlockSpec((1, S, NA_WIDTH), lambda b, i: (b, 0, 0)),
                  pl.BlockSpec((1, nrb, NA_WIDTH, tq), lambda b, i: (b, 0, 0, 0)),
                  _resident(bm)],
        out_specs=pl.BlockSpec((1, step, NA_WIDTH), lambda b, i: (b, i, 0)),
        compiler_params=pltpu.CompilerParams(
            dimension_semantics=("parallel", "arbitrary"), vmem_limit_bytes=VMEM_LIMIT),
        name="na_attn",
    )(q, k, vt, bm)


def _na_bias_table(rpb, rows):
    nrb = rows // NA_RB
    W = GRID_W
    H = rpb.shape[0]
    pad_c = W - NA_KC
    rp = jnp.pad(jnp.flip(rpb.astype(F32), axis=-1), ((0, 0), (0, 0), (pad_c, pad_c + 1)))
    skew = jnp.tile(rp, (1, 1, W))[:, :, :W * (2 * W - 1)].reshape(H, -1, W, 2 * W - 1)
    pad_lo, pad_hi = NA_KR // 2, NA_KROWS - NA_KR
    t = jnp.pad(skew[:, :, :, W - 1:] * LOG2E, ((0, 0), (pad_lo, pad_hi), (0, 0), (0, 0)))
    kri = np.arange(NA_KROWS)[:, None, None]
    kc = np.arange(W)[None, :, None]
    c = np.arange(W)[None, None, :]
    c0 = np.clip(c - NA_KC // 2, 0, W - NA_KC)
    col_ok = (kc >= c0) & (kc < c0 + NA_KC)
    variants = []
    for rb in (0, 1, nrb - 1):
        start = NA_RB * min(max(rb - 1, 0), nrb - 3)
        per_row = []
        for ri in range(NA_RB):
            r = NA_RB * rb + ri
            r0 = min(max(r - NA_KR // 2, 0), rows - NA_KR)
            d0 = start - r + (NA_KR - 1) + pad_lo
            assert 0 <= d0 and d0 + NA_KROWS <= t.shape[1]
            ok = ((start + kri >= r0) & (start + kri < r0 + NA_KR)) & col_ok
            per_row.append(jnp.where(ok[None], t[:, d0:d0 + NA_KROWS], NEG))
        variants.append(jnp.concatenate(per_row, axis=-1))
    return jnp.stack(variants, axis=1).reshape(H, 3, NA_KROWS * W, NA_RB * W)


def _outproj_body(x_ref, om_ref, on_ref, od_ref, gt_ref, gate_ref, w_ref, o_ref):
    def branch(o_ref_, a, b):
        mix = (o_ref_[0].astype(F32) * gt_ref[0, :, a:b].astype(F32)).astype(BF16)
        return _mm(mix, w_ref[a:b, :])

    y = (branch(om_ref, 0, MLA_WIDTH) + branch(on_ref, MLA_WIDTH, MLA_WIDTH + NA_WIDTH)
         + branch(od_ref, MLA_WIDTH + NA_WIDTH, D_MODEL))
    o_ref[0] = x_ref[0] + gate_ref[0, 2:3, :] * y


def _outproj(x, om, on, od, gt, mod_b, w_out, tm):
    B, S, _ = x.shape
    tok = lambda w: pl.BlockSpec((1, tm, w), lambda b, i: (b, i, 0))
    return pl.pallas_call(
        _outproj_body,
        out_shape=jax.ShapeDtypeStruct((B, S, D_MODEL), F32),
        grid=(B, S // tm),
        in_specs=[tok(D_MODEL), tok(MLA_WIDTH), tok(NA_WIDTH), tok(DIFF_WIDTH), tok(D_MODEL),
                  pl.BlockSpec((1, 3, D_MODEL), lambda b, i: (b, 0, 0)),
                  _resident(w_out)],
        out_specs=tok(D_MODEL),
        compiler_params=pltpu.CompilerParams(
            dimension_semantics=("parallel", "parallel"), vmem_limit_bytes=VMEM_LIMIT),
        name="outproj",
    )(x, om, on, od, gt, mod_b, w_out)


def _swap_halves(g, half):
    return jnp.concatenate([g[..., half:], g[..., :half]], axis=-1)


def _pad_cols(w, left, total):
    return jnp.pad(w, ((0, 0), (left, total - left - w.shape[-1])))


def _prep_layer(w_in, w_uq, w_ukv, norm_g, q_lat_g, kv_lat_g, mla_q_g, mla_k_g,
                na_q_g, na_k_g, diff_q_g, diff_k_g):
    sizes = (Q_LORA, KV_LORA, MLA_ROPE, 3 * NA_WIDTH, 3 * DIFF_WIDTH, D_MODEL)
    splits = np.cumsum(sizes)[:-1].tolist()
    w_cq, w_ckv, w_kpe, w_na, w_diff, w_gate = jnp.split(w_in, splits, axis=-1)
    hr = MLA_ROPE // 2
    hd = DIFF_QK // 2
    w_qd, w_kd, w_vd = jnp.split(w_diff, 3, axis=-1)
    w_nq, w_nk, w_nv = jnp.split(w_na, 3, axis=-1)
    w_ext = jnp.concatenate([
        w_cq, w_ckv,
        _pad_cols(w_kpe, MLA_NOPE, LANES), w_nq, w_nk, w_qd, w_kd, w_gate], axis=-1).astype(BF16)
    wnvt = w_nv.T.astype(BF16)
    wdvt = w_vd.T.astype(BF16)

    uq = w_uq.reshape(Q_LORA, MLA_HEADS, MLA_QK)
    wuq = jnp.pad(uq, ((0, 0), (0, 0), (0, LANES - MLA_QK))).reshape(Q_LORA, -1).astype(BF16)

    ukv = w_ukv.reshape(KV_LORA, MLA_HEADS, MLA_NOPE + MLA_V)
    wuk = jnp.pad(ukv[..., :MLA_NOPE], ((0, 0), (0, 0), (0, LANES - MLA_NOPE))).reshape(KV_LORA, -1).astype(BF16)
    wuvt = ukv[..., MLA_NOPE:].reshape(KV_LORA, -1).T.astype(BF16)

    def row(v):
        return jnp.pad(v.astype(F32), (0, D_MODEL - v.shape[0]))

    def mla_rows(g):
        rot = jnp.pad(_swap_halves(g[MLA_NOPE:], hr), (MLA_NOPE, 0))
        return row(g), row(rot)

    def diff_rows(g):
        n = DIFF_WIDTH // DIFF_QK
        return row(jnp.tile(g, n)), row(jnp.tile(_swap_halves(g, hd), n))

    rows = [row(norm_g), row(q_lat_g), row(kv_lat_g), *mla_rows(mla_q_g), *mla_rows(mla_k_g),
            row(jnp.tile(na_q_g, NA_HEADS)), row(jnp.tile(na_k_g, NA_HEADS)),
            *diff_rows(diff_q_g), *diff_rows(diff_k_g)]
    rows += [jnp.zeros((D_MODEL,), F32)] * (G_ROWS - len(rows))
    return w_ext, wuq, wuk, wuvt, wnvt, wdvt, jnp.stack(rows)


def _rope_tables(S):
    def cs(dim):
        inv = ROPE_THETA ** (-jnp.arange(0, dim, 2, dtype=F32) / dim)
        ang = jnp.arange(S, dtype=F32)[:, None] * inv[None, :]
        return jnp.cos(ang), jnp.sin(ang)

    c, s = cs(MLA_ROPE)
    pad = LANES - MLA_QK
    cm = jnp.concatenate([jnp.ones((S, MLA_NOPE), F32), c, c, jnp.zeros((S, pad), F32)], axis=-1)
    sm = jnp.concatenate([jnp.zeros((S, MLA_NOPE), F32), s, s, jnp.zeros((S, pad), F32)], axis=-1)
    c, s = cs(DIFF_QK)
    n = LANES // DIFF_QK
    cd = jnp.tile(jnp.concatenate([c, c], axis=-1), (1, n))
    sd = jnp.tile(jnp.concatenate([s, s], axis=-1), (1, n))
    return cm, sm, cd, sd


def _tile(S, pref):
    t = pref
    while S % t:
        t //= 2
    return t


def kernel(x, c, ada_w, ada_b, norm_g, w_in, q_lat_g, w_uq, kv_lat_g, w_ukv, mla_q_g, mla_k_g,
           na_q_g, na_k_g, na_rpb, diff_q_g, diff_k_g, lam_q1, lam_k1, lam_q2, lam_k2, subln_g, w_out):
    B, S, D = x.shape
    L = ada_w.shape[0]
    rows = S // GRID_W
    assert D == D_MODEL and S % (NA_STEP_BLOCKS * NA_RB * GRID_W) == 0 and rows >= NA_KROWS

    tabs = _rope_tables(S)
    mod = _adaln(c, ada_w, ada_b)
    tm = _tile(S, TILE_M)
    tk = _tile(S, TILE_K)
    tqd = _tile(S, TILE_Q_DIFF)

    weights = jax.vmap(_prep_layer)(w_in, w_uq, w_ukv, norm_g, q_lat_g, kv_lat_g, mla_q_g, mla_k_g,
                                    na_q_g, na_k_g, diff_q_g, diff_k_g)
    bias = jax.vmap(functools.partial(_na_bias_table, rows=rows))(na_rpb)
    mod_b = mod.transpose(0, 2, 1, 3)
    lamv = jnp.stack([lam_q1, lam_k1, lam_q2, lam_k2], axis=1).astype(F32)
    sgt = jnp.broadcast_to(subln_g.astype(F32)[:, :, None], (L, DIFF_V, tqd))
    w_out_b = w_out.astype(BF16)

    h = x
    for i in range(L):
        lam_init = 0.8 - 0.6 * math.exp(-0.3 * i)
        w_ext, wuq, wuk, wuvt, wnvt, wdvt, gv = (w[i] for w in weights)
        qm, km, vmt, nq, nk, nvt, dq, dk, dvt, gt = _inproj(h, mod_b[i], gv, w_ext, wuq, wuk, wuvt, wnvt, wdvt,
                                                            tabs, tm, tk)
        om = _mla_attn(qm, km, vmt, _tile(S, TILE_Q_MLA))
        on = _na_attn(nq, nk, nvt, bias[i])
        od = _diff_attn(dq, dk, dvt, lamv[i], sgt[i], lam_init, tqd)
        h = _outproj(h, om, on, od, gt, mod_b[i], w_out_b[i], tm)
    return h
```
